```python
import jax, jax.numpy as jnp
from jax import lax
import numpy as np

D_MODEL = 4096
BATCH = 2
SEQ = 8192
DEPTH = 4

HEAD_DIM = 64
D_RWKV = D_MODEL // 2
N_RWKV_HEADS = D_RWKV // HEAD_DIM
D_CONV = D_MODEL // 2
CONV_WIDTH = 3
R_DECAY = 128
R_ICLR = 128
R_VRES = 96
R_GATE = 480
D_FF = 2 * D_MODEL
RWKV_COLS = 3 * D_RWKV + R_DECAY + R_ICLR + R_GATE
CONV_COLS = 3 * D_CONV
GATE_COLS = 2 * D_MODEL
IN_COLS = RWKV_COLS + CONV_COLS + GATE_COLS
NORM_EPS = 1e-6
GN_EPS = 64e-5
DECAY_SCALE = 0.606531
L2_EPS = 1e-12

kernel_name = "hybrid_rwkv7_shortconv_convffn"


def rms_norm(x, g):
    xf = x.astype(jnp.float32)
    y = xf * lax.rsqrt(jnp.mean(xf * xf, axis=-1, keepdims=True) + NORM_EPS)
    return (y * g.astype(jnp.float32)).astype(x.dtype)


def token_shift(x):
    return jnp.pad(x, ((0, 0), (1, 0), (0, 0)))[:, :-1]


def causal_dwconv3(x, w):
    s = x.shape[1]
    xp = jnp.pad(x, ((0, 0), (CONV_WIDTH - 1, 0), (0, 0)))
    return w[0] * xp[:, 0:s] + w[1] * xp[:, 1:s + 1] + w[2] * xp[:, 2:s + 2]


def wkv7_scan(r, w, k, v, kk, a):
    b, s, h, n = r.shape

    def step(state, inp):
        r_t, w_t, k_t, v_t, kk_t, a_t = inp
        sa = jnp.einsum('bhvk,bhk->bhv', state, -kk_t)
        state = (state * w_t[:, :, None, :]
                 + sa[..., None] * (kk_t * a_t)[:, :, None, :]
                 + v_t[..., None] * k_t[:, :, None, :])
        out = jnp.einsum('bhvk,bhk->bhv', state, r_t)
        return state, out

    xs = tuple(jnp.moveaxis(t, 1, 0) for t in (r, w, k, v, kk, a))
    state0 = jnp.zeros((b, h, n, n), jnp.float32)
    _, out = lax.scan(step, state0, xs)
    return jnp.moveaxis(out, 0, 1)


def rwkv7_mix(p, v_first, vres, decay_up, decay_base, iclr_up, iclr_base, gate_up,
              k_scale, k_iclr, r_bonus, lnx_w, lnx_b):
    f32 = jnp.float32
    bsz, seq, _ = p.shape
    o3 = 3 * D_RWKV
    splits = [D_RWKV, 2 * D_RWKV, o3, o3 + R_DECAY, o3 + R_DECAY + R_ICLR]
    r, k, v, d_lo, a_lo, g_lo = jnp.split(p, splits, axis=-1)
    r, k, v = r.astype(f32), k.astype(f32), v.astype(f32)
    decay = jnp.exp(-DECAY_SCALE * jax.nn.sigmoid((decay_base + jnp.tanh(d_lo) @ decay_up).astype(f32)))
    iclr = jax.nn.sigmoid((iclr_base + a_lo @ iclr_up).astype(f32))
    out_gate = (jax.nn.sigmoid(g_lo) @ gate_up).astype(f32)
    if vres is None:
        v_first = v
    else:
        vd, vu, vb = vres
        v = v + (v_first - v) * jax.nn.sigmoid((vb + (v @ vd) @ vu).astype(f32))

    def heads(t):
        return t.reshape(bsz, seq, N_RWKV_HEADS, HEAD_DIM)

    kk = heads(k * k_scale)
    kk = kk / jnp.maximum(jnp.sqrt(jnp.sum(kk * kk, axis=-1, keepdims=True)), L2_EPS)
    k = k * (1.0 + (iclr - 1.0) * k_iclr)
    rh, kh, vh, wh, ah = heads(r), heads(k), heads(v), heads(decay), heads(iclr)
    o = wkv7_scan(rh, wh, kh, vh, kk, ah)
    mean = jnp.mean(o, axis=-1, keepdims=True)
    var = jnp.mean(jnp.square(o - mean), axis=-1, keepdims=True)
    o = ((o - mean) * lax.rsqrt(var + GN_EPS)).reshape(bsz, seq, D_RWKV) * lnx_w + lnx_b
    o = o + (jnp.sum(rh * kh * r_bonus, axis=-1, keepdims=True) * vh).reshape(bsz, seq, D_RWKV)
    return o * out_gate, v_first


def setup_inputs(seed: int = 0) -> dict:
    key = jax.random.key(seed)
    ks = jax.random.split(key, 32)
    f32 = jnp.float32
    L = DEPTH

    def nrm(k, shape, scale):
        return jax.random.normal(k, shape, f32) * scale

    return {
        "x": nrm(ks[0], (BATCH, SEQ, D_MODEL), 1.0),
        "attn_norm": 1.0 + nrm(ks[1], (L, D_MODEL), 0.02),
        "w_in": nrm(ks[2], (L, D_MODEL, IN_COLS), D_MODEL ** -0.5),
        "shift_mu": jax.random.uniform(ks[3], (L, RWKV_COLS), f32, 0.0, 1.0),
        "decay_up": nrm(ks[4], (L, R_DECAY, D_RWKV), R_DECAY ** -0.5),
        "decay_base": jax.random.uniform(ks[5], (L, D_RWKV), f32, -5.0, 1.0),
        "iclr_up": nrm(ks[6], (L, R_ICLR, D_RWKV), 0.5 * R_ICLR ** -0.5),
        "iclr_base": nrm(ks[7], (L, D_RWKV), 0.5),
        "gate_up": nrm(ks[8], (L, R_GATE, D_RWKV), R_GATE ** -0.5),
        "k_scale": 0.85 + nrm(ks[9], (L, D_RWKV), 0.05),
        "k_iclr": 1.0 + nrm(ks[10], (L, D_RWKV), 0.05),
        "r_bonus": nrm(ks[11], (L, N_RWKV_HEADS, HEAD_DIM), 0.1),
        "lnx_w": 1.0 + nrm(ks[12], (L, D_RWKV), 0.02),
        "lnx_b": nrm(ks[13], (L, D_RWKV), 0.01),
        "vres_down": nrm(ks[14], (L - 1, D_RWKV, R_VRES), D_RWKV ** -0.5),
        "vres_up": nrm(ks[15], (L - 1, R_VRES, D_RWKV), R_VRES ** -0.5),
        "vres_base": 1.0 + nrm(ks[16], (L - 1, D_RWKV), 0.1),
        "w_out_rwkv": nrm(ks[17], (L, D_RWKV, D_MODEL), D_RWKV ** -0.5),
        "sconv_w": nrm(ks[18], (L, CONV_WIDTH, D_CONV), CONV_WIDTH ** -0.5),
        "w_out_conv": nrm(ks[19], (L, D_CONV, D_MODEL), D_CONV ** -0.5),
        "w_o": nrm(ks[20], (L, D_MODEL, D_MODEL), D_MODEL ** -0.5),
        "ffn_norm": 1.0 + nrm(ks[21], (L, D_MODEL), 0.02),
        "w_up": nrm(ks[22], (L, D_MODEL, 2 * D_FF), D_MODEL ** -0.5),
        "ffn_conv_w": nrm(ks[23], (L, CONV_WIDTH, 2 * D_FF), CONV_WIDTH ** -0.5),
        "w_down": nrm(ks[24], (L, D_FF, D_MODEL), D_FF ** -0.5),
        "final_norm": 1.0 + nrm(ks[25], (D_MODEL,), 0.02),
    }


def reference(x, attn_norm, w_in, shift_mu, decay_up, decay_base, iclr_up, iclr_base,
              gate_up, k_scale, k_iclr, r_bonus, lnx_w, lnx_b, vres_down, vres_up,
              vres_base, w_out_rwkv, sconv_w, w_out_conv, w_o, ffn_norm, w_up,
              ffn_conv_w, w_down, final_norm):
    v_first = None
    for l in range(DEPTH):
        h = rms_norm(x, attn_norm[l])
        proj = h @ w_in[l]
        p_rwkv, p_conv, p_gate = jnp.split(proj, [RWKV_COLS, RWKV_COLS + CONV_COLS], axis=-1)

        p_rwkv = p_rwkv + shift_mu[l] * (token_shift(p_rwkv) - p_rwkv)
        vres = None if l == 0 else (vres_down[l - 1], vres_up[l - 1], vres_base[l - 1])
        o_rwkv, v_first = rwkv7_mix(p_rwkv, v_first, vres, decay_up[l], decay_base[l],
                                    iclr_up[l], iclr_base[l], gate_up[l], k_scale[l],
                                    k_iclr[l], r_bonus[l], lnx_w[l], lnx_b[l])
        y_rwkv = o_rwkv.astype(x.dtype) @ w_out_rwkv[l]

        b_gate, c_gate, u = jnp.split(p_conv, 3, axis=-1)
        y_conv = (b_gate * causal_dwconv3(c_gate * u, sconv_w[l])) @ w_out_conv[l]

        g_rwkv, g_conv = jnp.split(p_gate, 2, axis=-1)
        merged = jax.nn.sigmoid(g_rwkv) * y_rwkv + jax.nn.sigmoid(g_conv) * y_conv
        x = x + merged @ w_o[l]

        h = rms_norm(x, ffn_norm[l])
        hg, hu = jnp.split(causal_dwconv3(h @ w_up[l], ffn_conv_w[l]), 2, axis=-1)
        x = x + (jax.nn.silu(hg) * hu) @ w_down[l]
    return rms_norm(x, final_norm)
```

```python
import functools

import jax
import jax.numpy as jnp
from jax import lax
from jax.experimental import pallas as pl
from jax.experimental.pallas import tpu as pltpu

F32 = jnp.float32
BF16 = jnp.bfloat16

HEAD_DIM = 64
LANES = 128
CHUNK = 128
NORM_EPS = 1e-6
GN_EPS = 64e-5
DECAY_SCALE = 0.606531
L2_EPS = 1e-12
VMEM_LIMIT = 56 * 1024 * 1024


def _cparams(sem):
    return pltpu.CompilerParams(dimension_semantics=sem, vmem_limit_bytes=VMEM_LIMIT)


def _sigmoid(x):
    return 1.0 / (1.0 + jnp.exp(-x))


def _bdot(a, b):
    return jnp.dot(a.astype(BF16), b.astype(BF16), preferred_element_type=F32)


def _bdot_nt(a, b):
    return lax.dot_general(a.astype(BF16), b.astype(BF16), (((1,), (1,)), ((), ())),
                           preferred_element_type=F32)


def _bdot_tn(a, b):
    return lax.dot_general(a.astype(BF16), b.astype(BF16), (((0,), (0,)), ((), ())),
                           preferred_element_type=F32)


def _rmsnorm_kernel(x_ref, g_ref, o_ref):
    x = x_ref[...]
    ms = jnp.mean(x * x, axis=-1, keepdims=True)
    o_ref[...] = (x * lax.rsqrt(ms + NORM_EPS) * g_ref[...]).astype(o_ref.dtype)


def _rmsnorm(x, g, out_dtype, tm=256):
    m, d = x.shape
    return pl.pallas_call(
        _rmsnorm_kernel,
        grid=(m // tm,),
        in_specs=[pl.BlockSpec((tm, d), lambda i: (i, 0)),
                  pl.BlockSpec((1, d), lambda i: (0, 0))],
        out_specs=pl.BlockSpec((tm, d), lambda i: (i, 0)),
        out_shape=jax.ShapeDtypeStruct((m, d), out_dtype),
        compiler_params=_cparams(("parallel",)),
        name="rmsnorm",
    )(x, g.reshape(1, d))


def _mm_kernel(a_ref, w_ref, *rest, nk, epilogue, n_extra):
    extra = rest[:n_extra]
    o_ref = rest[n_extra]
    if nk == 1:
        acc = jnp.dot(a_ref[...], w_ref[...], preferred_element_type=F32)
        o_ref[...] = epilogue(acc, *[e[...] for e in extra]).astype(o_ref.dtype)
    else:
        acc_ref = rest[n_extra + 1]
        k = pl.program_id(2)

        @pl.when(k == 0)
        def _():
            acc_ref[...] = jnp.zeros_like(acc_ref)

        acc_ref[...] += jnp.dot(a_ref[...], w_ref[...], preferred_element_type=F32)

        @pl.when(k == nk - 1)
        def _():
            o_ref[...] = epilogue(acc_ref[...], *[e[...] for e in extra]).astype(o_ref.dtype)


def _matmul(a, w, out_dtype, *, tm, tn, tk=None, extras=(), epilogue=None, name="mm"):
    m, kdim = a.shape
    n = w.shape[1]
    tk = kdim if tk is None else tk
    nk = kdim // tk
    if epilogue is None:
        epilogue = lambda acc: acc
    kern = functools.partial(_mm_kernel, nk=nk, epilogue=epilogue, n_extra=len(extras))
    in_specs = [pl.BlockSpec((tm, tk), lambda i, j, k: (i, k)),
                pl.BlockSpec((tk, tn), lambda i, j, k: (k, j))]
    in_specs += [pl.BlockSpec((tm, tn), lambda i, j, k: (i, j)) for _ in extras]
    scratch = [pltpu.VMEM((tm, tn), F32)] if nk > 1 else []
    return pl.pallas_call(
        kern,
        grid=(m // tm, n // tn, nk),
        in_specs=in_specs,
        out_specs=pl.BlockSpec((tm, tn), lambda i, j, k: (i, j)),
        out_shape=jax.ShapeDtypeStruct((m, n), out_dtype),
        scratch_shapes=scratch,
        compiler_params=_cparams(("parallel", "parallel", "arbitrary")),
        name=name,
    )(a, w, *extras)


def _shift_rows(x, carry, nshift):
    rows = lax.broadcasted_iota(jnp.int32, x.shape, 0)
    out = pltpu.roll(x, nshift, 0)
    for r in range(nshift):
        src = carry[8 - nshift + r:8 - nshift + r + 1, :]
        out = jnp.where(rows == r, src, out)
    return out


def _conv3(x, carry, w_ref):
    x1 = _shift_rows(x, carry, 1)
    x2 = _shift_rows(x, carry, 2)
    return w_ref[0:1, :] * x2 + w_ref[1:2, :] * x1 + w_ref[2:3, :] * x


def _mm_conv_kernel(a_ref, w_ref, cw_ref, o_ref, carry_ref, *, nparts, tc, tiles_per_seq):
    i = pl.program_id(1)
    acc = jnp.dot(a_ref[...], w_ref[...], preferred_element_type=F32)
    tm = acc.shape[0]
    first = (i % tiles_per_seq) == 0
    if nparts == 3:
        b = acc[:, 0:tc]
        pre = acc[:, tc:2 * tc] * acc[:, 2 * tc:3 * tc]
    else:
        pre = acc
    carry = jnp.where(first, 0.0, carry_ref[...])
    y = _conv3(pre, carry, cw_ref)
    carry_ref[...] = pre[tm - 8:tm, :]
    if nparts == 3:
        out = b * y
    else:
        g = y[:, 0:tc]
        out = g * _sigmoid(g) * y[:, tc:2 * tc]
    o_ref[...] = out.astype(o_ref.dtype)


def _matmul_conv(a, w, cw, *, nparts, tc, tm, seq, name):
    m, kdim = a.shape
    n = w.shape[1]
    tn = nparts * tc
    ncw = 1 if nparts == 3 else 2
    kern = functools.partial(_mm_conv_kernel, nparts=nparts, tc=tc, tiles_per_seq=seq // tm)
    return pl.pallas_call(
        kern,
        grid=(n // tn, m // tm),
        in_specs=[pl.BlockSpec((tm, kdim), lambda j, i: (i, 0)),
                  pl.BlockSpec((kdim, tn), lambda j, i: (0, j)),
                  pl.BlockSpec((3, ncw * tc), lambda j, i: (0, j))],
        out_specs=pl.BlockSpec((tm, tc), lambda j, i: (i, j)),
        out_shape=jax.ShapeDtypeStruct((m, (n // tn) * tc), BF16),
        scratch_shapes=[pltpu.VMEM((8, ncw * tc), F32)],
        compiler_params=_cparams(("arbitrary", "arbitrary")),
        name=name,
    )(a, w, cw)


def _merge_kernel(o_ref, z_ref, wr_ref, wc_ref, gr_ref, gc_ref, out_ref):
    yr = jnp.dot(o_ref[...], wr_ref[...], preferred_element_type=F32)
    yc = jnp.dot(z_ref[...], wc_ref[...], preferred_element_type=F32)
    out = gr_ref[...].astype(F32) * yr + gc_ref[...].astype(F32) * yc
    out_ref[...] = out.astype(out_ref.dtype)


def _merge(o, z, wr, wc, gates, *, tm, tn):
    m, kdim = o.shape
    n = wr.shape[1]
    nj = n // tn
    return pl.pallas_call(
        _merge_kernel,
        grid=(m // tm, nj),
        in_specs=[pl.BlockSpec((tm, kdim), lambda i, j: (i, 0)),
                  pl.BlockSpec((tm, kdim), lambda i, j: (i, 0)),
                  pl.BlockSpec((kdim, tn), lambda i, j: (0, j)),
                  pl.BlockSpec((kdim, tn), lambda i, j: (0, j)),
                  pl.BlockSpec((tm, tn), lambda i, j: (i, j)),
                  pl.BlockSpec((tm, tn), lambda i, j: (i, j + nj))],
        out_specs=pl.BlockSpec((tm, tn), lambda i, j: (i, j)),
        out_shape=jax.ShapeDtypeStruct((m, n), BF16),
        compiler_params=_cparams(("parallel", "parallel")),
        name="merge",
    )(o, z, wr, wc, gates, gates)


def _prep_kernel(*refs, d_rwkv, r_dec, r_icl, r_gate_p, has_vres, tiles_per_seq):
    if has_vres:
        (p_ref, mu_ref, dup_ref, dbase_ref, iup_ref, ibase_ref, gup_ref,
         vd_ref, vu_ref, vb_ref, vf_ref,
         r_out, k_out, v_out, lw_out, a_out, g_out, carry_ref) = refs
    else:
        (p_ref, mu_ref, dup_ref, dbase_ref, iup_ref, ibase_ref, gup_ref,
         r_out, k_out, v_out, lw_out, a_out, g_out, carry_ref) = refs
    i = pl.program_id(0)
    first = (i % tiles_per_seq) == 0
    tp = p_ref.shape[0]
    carry = jnp.where(first, 0.0, carry_ref[...])

    def lerp(c0, c1):
        p = p_ref[:, c0:c1]
        prev = _shift_rows(p, carry[:, c0:c1], 1)
        return p + mu_ref[:, c0:c1] * (prev - p)

    d = d_rwkv
    o_d = 3 * d
    o_a = o_d + r_dec
    o_g = o_a + r_icl
    r_out[...] = lerp(0, d)
    k_out[...] = lerp(d, 2 * d)
    v = lerp(2 * d, 3 * d)
    d_lo = lerp(o_d, o_a)
    a_lo = lerp(o_a, o_g)
    g_lo = lerp(o_g, o_g + r_gate_p)
    carry_ref[...] = p_ref[tp - 8:tp, :]

    dec = dbase_ref[...] + _bdot(jnp.tanh(d_lo), dup_ref[...])
    lw_out[...] = -DECAY_SCALE * _sigmoid(dec)
    a_out[...] = _sigmoid(ibase_ref[...] + _bdot(a_lo, iup_ref[...]))
    g_out[...] = _bdot(_sigmoid(g_lo), gup_ref[...]).astype(g_out.dtype)
    if has_vres:
        lo = _bdot(v, vd_ref[...])
        mix = _sigmoid(vb_ref[...] + _bdot(lo, vu_ref[...]))
        v = v + (vf_ref[...] - v) * mix
    v_out[...] = v


def _rwkv_prep(p, mu, dup, dbase, iup, ibase, gup, vres, v_first, *, d_rwkv, seq, tp=128):
    m, ncol = p.shape
    r_dec, r_icl, r_gate_p = dup.shape[0], iup.shape[0], gup.shape[0]
    has_vres = vres is not None
    row = lambda a: a.reshape(1, -1)
    full = lambda a: pl.BlockSpec(a.shape, lambda i: (0, 0))
    tile = pl.BlockSpec((tp, d_rwkv), lambda i: (i, 0))
    args = [p, row(mu), dup, row(dbase), iup, row(ibase), gup]
    in_specs = [pl.BlockSpec((tp, ncol), lambda i: (i, 0))] + [full(a) for a in args[1:]]
    if has_vres:
        vd, vu, vb = vres
        extra = [vd, vu, row(vb)]
        args += extra + [v_first]
        in_specs += [full(a) for a in extra] + [tile]
    kern = functools.partial(_prep_kernel, d_rwkv=d_rwkv, r_dec=r_dec, r_icl=r_icl,
                             r_gate_p=r_gate_p, has_vres=has_vres, tiles_per_seq=seq // tp)
    f32o = jax.ShapeDtypeStruct((m, d_rwkv), F32)
    return pl.pallas_call(
        kern,
        grid=(m // tp,),
        in_specs=in_specs,
        out_specs=[tile] * 6,
        out_shape=[f32o, f32o, f32o, f32o, f32o, jax.ShapeDtypeStruct((m, d_rwkv), BF16)],
        scratch_shapes=[pltpu.VMEM((8, ncol), F32)],
        compiler_params=_cparams(("arbitrary",)),
        name="rwkv_prep",
    )(*args)


def _split_bf16(x, parts):
    out = []
    for _ in range(parts):
        h = x.astype(BF16)
        out.append(h)
        x = x - h.astype(F32)
    return out


def _wkv_chunk(r, k, v, lw, a, gate, ks, ki, rb, lnw, lnb, state):
    c = r.shape[0]
    hc = HEAD_DIM
    lane = lax.broadcasted_iota(jnp.int32, (c, LANES), 1)
    head0 = lane < hc

    bo_r = lax.broadcasted_iota(jnp.int32, (2 * LANES, LANES), 0)
    bo_c = lax.broadcasted_iota(jnp.int32, (2 * LANES, LANES), 1)
    block_ones = (((bo_r & (LANES - 1)) < hc) == (bo_c < hc)).astype(BF16)

    def head_sum(x):
        hi, lo = _split_bf16(x, 2)
        return jnp.dot(jnp.concatenate([hi, lo], axis=1), block_ones, preferred_element_type=F32)

    kk = k * ks
    kk = kk / jnp.maximum(jnp.sqrt(head_sum(kk * kk)), L2_EPS)
    kmod = k * (1.0 + (a - 1.0) * ki)
    b = kk * a

    tri_r = lax.broadcasted_iota(jnp.int32, (c, 3 * c), 0)
    tri_c = lax.broadcasted_iota(jnp.int32, (c, 3 * c), 1)
    tri3 = ((tri_c & (c - 1)) <= tri_r).astype(BF16)
    cum = jnp.dot(tri3, jnp.concatenate(_split_bf16(lw, 3), axis=0), preferred_element_type=F32)
    cmid = cum[c // 2 - 1:c // 2, :]
    cend = cum[c - 1:c, :]
    e_t = jnp.exp(cum - cmid)
    r_s = r * e_t
    a_s = -kk * jnp.exp(cum - lw - cmid)
    e_inv = jnp.exp(cmid - cum)
    b_s = b * e_inv
    k_s = kmod * e_inv

    def heads_stacked(x):
        return jnp.concatenate([jnp.where(head0[:x.shape[0]], x, 0.0),
                                jnp.where(head0[:x.shape[0]], 0.0, x)], axis=0)

    t_i = lax.broadcasted_iota(jnp.int32, (c, 2 * c), 0)
    j_i = lax.broadcasted_iota(jnp.int32, (c, 2 * c), 1)
    jm = j_i & (c - 1)
    left = j_i < c

    def bd(n):
        return jnp.concatenate([jnp.where(left, n, 0.0), jnp.where(left, 0.0, n)], axis=0)

    lhs = jnp.concatenate([a_s, r_s], axis=0)
    rhs = jnp.concatenate([heads_stacked(b_s), heads_stacked(k_s)], axis=0)
    t1 = _bdot_nt(lhs, rhs)
    strict = jm < t_i
    incl = jm <= t_i
    a_ab = jnp.where(strict, t1[0:c, 0:2 * c], 0.0)
    a_ak = jnp.where(strict, t1[0:c, 2 * c:4 * c], 0.0)
    a_rb = jnp.where(incl, t1[c:2 * c, 0:2 * c], 0.0)
    a_rk = jnp.where(incl, t1[c:2 * c, 2 * c:4 * c], 0.0)

    diff = t_i ^ jm
    eye = (jm == t_i).astype(F32)
    a_d = jnp.where(diff < 8, a_ab, 0.0)
    a2 = _bdot(a_d, bd(a_d))
    x1 = eye + a_d
    sq = _bdot(jnp.concatenate([x1, a2], axis=0), bd(a2))
    x2 = x1 + sq[0:c]
    inv = x2 + _bdot(x2, bd(sq[c:2 * c]))
    s = 8
    while s < c:
        a_off = jnp.where((diff >> (s.bit_length() - 1)) == 1, a_ab, 0.0)
        inv = inv + _bdot(_bdot(inv, bd(a_off)), bd(inv))
        s *= 2

    yo = _bdot(jnp.concatenate([a_ak, a_rk], axis=0), heads_stacked(v))
    y = yo[0:c]
    o_loc = yo[c:2 * c]
    rhs_w = jnp.concatenate(
        [jnp.concatenate([jnp.where(head0, a_s, 0.0), jnp.where(head0, y, 0.0)], axis=1),
         jnp.concatenate([jnp.where(head0, 0.0, a_s), jnp.where(head0, 0.0, y)], axis=1)], axis=0)
    w12 = _bdot(inv, rhs_w)
    w1 = w12[:, 0:LANES]
    w2 = w12[:, LANES:2 * LANES]

    s_mid = state * jnp.exp(cmid)
    d1 = _bdot_nt(jnp.concatenate([w1, r_s], axis=0), s_mid)
    u = d1[0:c] + w2
    o = d1[c:2 * c] + _bdot(a_rb, heads_stacked(u)) + o_loc
    g = _bdot_tn(jnp.concatenate([u, v], axis=0), jnp.concatenate([b_s, k_s], axis=0))
    sr = lax.broadcasted_iota(jnp.int32, (LANES, LANES), 0)
    sc = lax.broadcasted_iota(jnp.int32, (LANES, LANES), 1)
    g = jnp.where((sr < hc) == (sc < hc), g, 0.0)
    new_state = (s_mid + g) * jnp.exp(cend - cmid)

    inv_n = 1.0 / hc
    mean = head_sum(o) * inv_n
    dlt = o - mean
    var = head_sum(dlt * dlt) * inv_n
    on = dlt * lax.rsqrt(var + GN_EPS) * lnw + lnb
    on = on + head_sum(r * kmod * rb) * v
    return on * gate, new_state


def _wkv_kernel(r_ref, k_ref, v_ref, lw_ref, a_ref, g_ref, ks_ref, ki_ref, rb_ref, lnw_ref,
                lnb_ref, o_ref, state_ref, *, nchunk):
    @pl.when(pl.program_id(2) == 0)
    def _():
        state_ref[...] = jnp.zeros_like(state_ref)

    state = state_ref[...]
    for ci in range(nchunk):
        sl = pl.ds(ci * CHUNK, CHUNK)
        out, state = _wkv_chunk(r_ref[sl, :], k_ref[sl, :], v_ref[sl, :], lw_ref[sl, :],
                                a_ref[sl, :], g_ref[sl, :].astype(F32), ks_ref[...], ki_ref[...],
                                rb_ref[...], lnw_ref[...], lnb_ref[...], state)
        o_ref[sl, :] = out.astype(o_ref.dtype)
    state_ref[...] = state


def _wkv(r, k, v, lw, a, gate, ks, ki, rb, lnw, lnb, *, batch, seq, nchunk=2):
    m, d = r.shape
    rows = nchunk * CHUNK
    nblk = seq // rows
    tile = pl.BlockSpec((rows, LANES), lambda b, h, c: (b * nblk + c, h))
    par = pl.BlockSpec((1, LANES), lambda b, h, c: (0, h))
    row = lambda x: x.reshape(1, d)
    return pl.pallas_call(
        functools.partial(_wkv_kernel, nchunk=nchunk),
        grid=(batch, d // LANES, nblk),
        in_specs=[tile] * 6 + [par] * 5,
        out_specs=tile,
        out_shape=jax.ShapeDtypeStruct((m, d), BF16),
        scratch_shapes=[pltpu.VMEM((LANES, LANES), F32)],
        compiler_params=_cparams(("parallel", "parallel", "arbitrary")),
        name="wkv7",
    )(r, k, v, lw, a, gate, row(ks), row(ki), row(rb), row(lnw), row(lnb))


def _group_columns(w, nparts, tc):
    lead = w.shape[:-1]
    n = w.shape[-1] // nparts
    w = w.reshape(*lead, nparts, n // tc, tc)
    w = jnp.swapaxes(w, -3, -2)
    return w.reshape(*lead, nparts * n)


def _pad_to(x, size, axis):
    pad = size - x.shape[axis]
    if pad == 0:
        return x
    widths = [(0, 0)] * x.ndim
    widths[axis] = (0, pad)
    return jnp.pad(x, widths)


def _round_up(n, mult):
    return (n + mult - 1) // mult * mult


def _pick(n, prefs):
    for p in prefs:
        if n % p == 0:
            return p
    return n


def kernel(x, attn_norm, w_in, shift_mu, decay_up, decay_base, iclr_up, iclr_base, gate_up, k_scale, k_iclr, r_bonus, lnx_w, lnx_b, vres_down, vres_up, vres_base, w_out_rwkv, sconv_w, w_out_conv, w_o, ffn_norm, w_up, ffn_conv_w, w_down, final_norm):
    batch, seq, d_model = x.shape
    depth = w_in.shape[0]
    d_rwkv = decay_up.shape[2]
    d_conv = sconv_w.shape[2]
    d_ff = w_down.shape[1]
    r_dec, r_icl, r_gate, r_vres = decay_up.shape[1], iclr_up.shape[1], gate_up.shape[1], vres_down.shape[2]
    rwkv_cols = 3 * d_rwkv + r_dec + r_icl + r_gate
    conv_cols = 3 * d_conv
    m = batch * seq

    r_gate_p = _round_up(r_gate, LANES)
    r_vres_p = _round_up(r_vres, LANES)
    rwkv_cols_p = 3 * d_rwkv + r_dec + r_icl + r_gate_p

    tm = _pick(seq, (1024, 512, 256, 128))
    tn_r = _pick(rwkv_cols_p, (768, 1152, 1024, 512, 256, 128))
    tn_g = _pick(2 * d_model, (1024, 512, 256, 128))
    tn_d = _pick(d_model, (1024, 512, 256, 128))
    tc_conv = _pick(d_conv, (512, 256, 128))
    tc_ff = _pick(d_ff, (512, 256, 128))
    tk_down = _pick(d_ff, (2048, 1024, 512, 256, 128))
    tp = _pick(seq, (128,))

    xf = x.reshape(m, d_model)
    h = _rmsnorm(xf, attn_norm[0], BF16)
    v_first = None
    for l in range(depth):
        wl = w_in[l]
        w_r = _pad_to(wl[:, :rwkv_cols], rwkv_cols_p, 1).astype(BF16)
        w_c = _group_columns(wl[:, rwkv_cols:rwkv_cols + conv_cols], 3, tc_conv).astype(BF16)
        w_g = wl[:, rwkv_cols + conv_cols:].astype(BF16)

        p_r = _matmul(h, w_r, F32, tm=tm, tn=tn_r, name="proj_rwkv")
        vres = None
        if l > 0:
            vres = (_pad_to(vres_down[l - 1], r_vres_p, 1).astype(BF16),
                    _pad_to(vres_up[l - 1], r_vres_p, 0).astype(BF16), vres_base[l - 1])
        r, k, v, lw, a, gate = _rwkv_prep(
            p_r, _pad_to(shift_mu[l], rwkv_cols_p, 0), decay_up[l].astype(BF16), decay_base[l],
            iclr_up[l].astype(BF16), iclr_base[l], _pad_to(gate_up[l], r_gate_p, 0).astype(BF16),
            vres, v_first, d_rwkv=d_rwkv, seq=seq, tp=tp)
        if l == 0:
            v_first = v
        o_rwkv = _wkv(r, k, v, lw, a, gate, k_scale[l], k_iclr[l], r_bonus[l].reshape(-1),
                      lnx_w[l], lnx_b[l], batch=batch, seq=seq)

        z_conv = _matmul_conv(h, w_c, sconv_w[l], nparts=3, tc=tc_conv, tm=tm, seq=seq,
                              name="proj_conv")

        gates = _matmul(h, w_g, BF16, tm=tm, tn=tn_g, epilogue=_sigmoid, name="proj_gates")
        merged = _merge(o_rwkv, z_conv, w_out_rwkv[l].astype(BF16), w_out_conv[l].astype(BF16),
                        gates, tm=tm, tn=tn_d)
        xf = _matmul(merged, w_o[l].astype(BF16), F32, tm=tm, tn=tn_d, extras=(xf,),
                     epilogue=lambda acc, res: res + acc, name="w_o")

        h = _rmsnorm(xf, ffn_norm[l], BF16)
        w_u = _group_columns(w_up[l], 2, tc_ff).astype(BF16)
        cw_u = _group_columns(ffn_conv_w[l], 2, tc_ff)
        act = _matmul_conv(h, w_u, cw_u, nparts=2, tc=tc_ff, tm=tm, seq=seq, name="ffn_up")
        xf = _matmul(act, w_down[l].astype(BF16), F32, tm=tm, tn=tn_d, tk=tk_down, extras=(xf,),
                     epilogue=lambda acc, res: res + acc, name="w_down")
        if l + 1 < depth:
            h = _rmsnorm(xf, attn_norm[l + 1], BF16)
    out = _rmsnorm(xf, final_norm, F32)
    return out.reshape(batch, seq, d_model)
```

```python
import functools

import jax
import jax.numpy as jnp
from jax import lax
from jax.experimental import pallas as pl
from jax.experimental.pallas import tpu as pltpu

F32 = jnp.float32
BF16 = jnp.bfloat16

HEAD_DIM = 64
LANES = 128
WKV_HEADS = 4
WKV_CHUNK = 64
NORM_EPS = 1e-6
GN_EPS = 64e-5
DECAY_SCALE = 0.606531
L2_EPS = 1e-12
VMEM_LIMIT = 56 * 1024 * 1024


def _cparams(sem):
    return pltpu.CompilerParams(dimension_semantics=sem, vmem_limit_bytes=VMEM_LIMIT)


def _sigmoid(x):
    return 1.0 / (1.0 + jnp.exp(-x))


def _bdot(a, b):
    return jnp.dot(a.astype(BF16), b.astype(BF16), preferred_element_type=F32)


def _bdot_nt(a, b):
    return lax.dot_general(a.astype(BF16), b.astype(BF16), (((1,), (1,)), ((), ())),
                           preferred_element_type=F32)


def _bdot_tn(a, b):
    return lax.dot_general(a.astype(BF16), b.astype(BF16), (((0,), (0,)), ((), ())),
                           preferred_element_type=F32)


def _rmsnorm_kernel(x_ref, g_ref, o_ref):
    x = x_ref[...]
    ms = jnp.mean(x * x, axis=-1, keepdims=True)
    o_ref[...] = (x * lax.rsqrt(ms + NORM_EPS) * g_ref[...]).astype(o_ref.dtype)


def _rmsnorm(x, g, out_dtype, tm=256):
    m, d = x.shape
    return pl.pallas_call(
        _rmsnorm_kernel,
        grid=(m // tm,),
        in_specs=[pl.BlockSpec((tm, d), lambda i: (i, 0)),
                  pl.BlockSpec((1, d), lambda i: (0, 0))],
        out_specs=pl.BlockSpec((tm, d), lambda i: (i, 0)),
        out_shape=jax.ShapeDtypeStruct((m, d), out_dtype),
        compiler_params=_cparams(("parallel",)),
        name="rmsnorm",
    )(x, g.reshape(1, d))


def _row_scale(ssq, inv_d):
    return lax.rsqrt(jnp.sum(ssq, axis=-1, keepdims=True) * inv_d + NORM_EPS)


def _lane_partial_ssq(x):
    sq = x * x
    out = sq[:, 0:LANES]
    for c in range(1, x.shape[1] // LANES):
        out = out + sq[:, c * LANES:(c + 1) * LANES]
    return out


def _residual_epilogue(acc, res, gain):
    x = res + acc
    return x, x * gain, _lane_partial_ssq(x)


def _prenorm_kernel(x_ref, g_ref, xg_ref, ssq_ref):
    x = x_ref[...]
    xg_ref[...] = (x * g_ref[...]).astype(xg_ref.dtype)
    ssq_ref[...] = _lane_partial_ssq(x)


def _prenorm(x, g, tm=256):
    m, d = x.shape
    return pl.pallas_call(
        _prenorm_kernel,
        grid=(m // tm,),
        in_specs=[pl.BlockSpec((tm, d), lambda i: (i, 0)),
                  pl.BlockSpec((1, d), lambda i: (0, 0))],
        out_specs=[pl.BlockSpec((tm, d), lambda i: (i, 0)),
                   pl.BlockSpec((tm, LANES), lambda i: (i, 0))],
        out_shape=[jax.ShapeDtypeStruct((m, d), BF16), jax.ShapeDtypeStruct((m, LANES), F32)],
        compiler_params=_cparams(("parallel",)),
        name="prenorm",
    )(x, g.reshape(1, d))


def _accumulate_over_columns(o_ref, val):
    j = pl.program_id(1)

    @pl.when(j == 0)
    def _():
        o_ref[...] = val

    @pl.when(j != 0)
    def _():
        o_ref[...] += val


def _mm_kernel(a_ref, w_ref, *rest, nk, epilogue, n_tile, n_row, acc_outs, inv_d):
    n_out = len(acc_outs)
    n_ssq = 0 if inv_d is None else 1
    ssq_ref = rest[0] if n_ssq else None
    tiles = rest[n_ssq:n_ssq + n_tile]
    rows = rest[n_ssq + n_tile:n_ssq + n_tile + n_row]
    outs = rest[n_ssq + n_tile + n_row:n_ssq + n_tile + n_row + n_out]

    def finish(acc):
        if n_ssq:
            acc = acc * _row_scale(ssq_ref[...], inv_d)
        vals = epilogue(acc, *[e[...] for e in tiles], *[e[...] for e in rows])
        vals = vals if isinstance(vals, tuple) else (vals,)
        for o_ref, val, accumulate in zip(outs, vals, acc_outs):
            if accumulate:
                _accumulate_over_columns(o_ref, val)
            else:
                o_ref[...] = val.astype(o_ref.dtype)

    if nk == 1:
        finish(jnp.dot(a_ref[...], w_ref[...], preferred_element_type=F32))
    else:
        acc_ref = rest[-1]
        k = pl.program_id(2)

        @pl.when(k == 0)
        def _():
            acc_ref[...] = jnp.zeros_like(acc_ref)

        acc_ref[...] += jnp.dot(a_ref[...], w_ref[...], preferred_element_type=F32)

        @pl.when(k == nk - 1)
        def _():
            finish(acc_ref[...])


def _matmul(a, w, out_dtypes, *, tm, tn, tk=None, ssq=None, tile_extras=(), row_extras=(),
            epilogue=None, name="mm"):
    m, kdim = a.shape
    n = w.shape[1]
    tk = kdim if tk is None else tk
    nk = kdim // tk
    nj = n // tn
    if epilogue is None:
        epilogue = lambda acc: acc
    single = not isinstance(out_dtypes, (tuple, list))
    out_dtypes = (out_dtypes,) if single else tuple(out_dtypes)
    inv_d = None if ssq is None else 1.0 / kdim
    acc_outs = tuple(dt == "ssq" for dt in out_dtypes)
    kern = functools.partial(_mm_kernel, nk=nk, epilogue=epilogue, n_tile=len(tile_extras),
                             n_row=len(row_extras), acc_outs=acc_outs, inv_d=inv_d)
    in_specs = [pl.BlockSpec((tm, tk), lambda i, j, k: (i, k)),
                pl.BlockSpec((tk, tn), lambda i, j, k: (k, j))]
    args = [a, w]
    if ssq is not None:
        in_specs.append(pl.BlockSpec((tm, ssq.shape[1]), lambda i, j, k: (i, 0)))
        args.append(ssq)
    in_specs += [pl.BlockSpec((tm, tn), lambda i, j, k: (i, j)) for _ in tile_extras]
    in_specs += [pl.BlockSpec((1, tn), lambda i, j, k: (0, j)) for _ in row_extras]
    args += list(tile_extras) + [r.reshape(1, n) for r in row_extras]
    out_specs, out_shape = [], []
    for dt in out_dtypes:
        if dt == "ssq":
            out_specs.append(pl.BlockSpec((tm, LANES), lambda i, j, k: (i, 0)))
            out_shape.append(jax.ShapeDtypeStruct((m, LANES), F32))
        else:
            out_specs.append(pl.BlockSpec((tm, tn), lambda i, j, k: (i, j)))
            out_shape.append(jax.ShapeDtypeStruct((m, n), dt))
    col_sem = "arbitrary" if any(acc_outs) else "parallel"
    scratch = [pltpu.VMEM((tm, tn), F32)] if nk > 1 else []
    outs = pl.pallas_call(
        kern,
        grid=(m // tm, nj, nk),
        in_specs=in_specs,
        out_specs=out_specs,
        out_shape=out_shape,
        scratch_shapes=scratch,
        compiler_params=_cparams(("parallel", col_sem, "arbitrary")),
        name=name,
    )(*args)
    return outs[0] if single else outs


def _shift_rows(x, carry, nshift):
    rows = lax.broadcasted_iota(jnp.int32, x.shape, 0)
    out = pltpu.roll(x, nshift, 0)
    for r in range(nshift):
        src = carry[8 - nshift + r:8 - nshift + r + 1, :]
        out = jnp.where(rows == r, src, out)
    return out


def _conv3(x, carry, w_ref):
    x1 = _shift_rows(x, carry, 1)
    x2 = _shift_rows(x, carry, 2)
    return w_ref[0:1, :] * x2 + w_ref[1:2, :] * x1 + w_ref[2:3, :] * x


def _mm_conv_kernel(*refs, nparts, tiles_per_seq, inv_d):
    a_ref, ssq_ref = refs[0], refs[1]
    refs = refs[2:]
    w_refs = refs[0:nparts]
    ncw = 1 if nparts == 3 else 2
    cw_refs = refs[nparts:nparts + ncw]
    o_ref = refs[nparts + ncw]
    carry_refs = refs[1 + nparts + ncw:]
    i = pl.program_id(1)
    first = (i % tiles_per_seq) == 0
    a = a_ref[...]
    scale = _row_scale(ssq_ref[...], inv_d)
    accs = [jnp.dot(a, w[...], preferred_element_type=F32) * scale for w in w_refs]
    tm = accs[0].shape[0]
    if nparts == 3:
        pres = [accs[1] * accs[2]]
    else:
        pres = accs
    ys = []
    for pre, cw_ref, carry_ref in zip(pres, cw_refs, carry_refs):
        carry = jnp.where(first, 0.0, carry_ref[...])
        ys.append(_conv3(pre, carry, cw_ref))
        carry_ref[...] = pre[tm - 8:tm, :]
    if nparts == 3:
        out = accs[0] * ys[0]
    else:
        out = ys[0] * _sigmoid(ys[0]) * ys[1]
    o_ref[...] = out.astype(o_ref.dtype)


def _matmul_conv(a, ssq, w, cw, *, nparts, tc, tm, seq, name):
    m, kdim = a.shape
    n = w.shape[1] // nparts
    nj = n // tc
    ncw = 1 if nparts == 3 else 2
    kern = functools.partial(_mm_conv_kernel, nparts=nparts, tiles_per_seq=seq // tm,
                             inv_d=1.0 / kdim)
    w_specs = [pl.BlockSpec((kdim, tc), lambda j, i, p=p: (0, j + p * nj)) for p in range(nparts)]
    cw_specs = [pl.BlockSpec((3, tc), lambda j, i, p=p: (0, j + p * nj)) for p in range(ncw)]
    return pl.pallas_call(
        kern,
        grid=(nj, m // tm),
        in_specs=[pl.BlockSpec((tm, kdim), lambda j, i: (i, 0)),
                  pl.BlockSpec((tm, ssq.shape[1]), lambda j, i: (i, 0))] + w_specs + cw_specs,
        out_specs=pl.BlockSpec((tm, tc), lambda j, i: (i, j)),
        out_shape=jax.ShapeDtypeStruct((m, n), BF16),
        scratch_shapes=[pltpu.VMEM((8, tc), F32)] * ncw,
        compiler_params=_cparams(("arbitrary", "arbitrary")),
        name=name,
    )(a, ssq, *([w] * nparts), *([cw] * ncw))


def _merge_kernel(o_ref, z_ref, wr_ref, wc_ref, gr_ref, gc_ref, out_ref):
    yr = jnp.dot(o_ref[...], wr_ref[...], preferred_element_type=F32)
    yc = jnp.dot(z_ref[...], wc_ref[...], preferred_element_type=F32)
    out = gr_ref[...].astype(F32) * yr + gc_ref[...].astype(F32) * yc
    out_ref[...] = out.astype(out_ref.dtype)


def _merge(o, z, wr, wc, gates, *, tm, tn):
    m, kdim = o.shape
    n = wr.shape[1]
    nj = n // tn
    return pl.pallas_call(
        _merge_kernel,
        grid=(m // tm, nj),
        in_specs=[pl.BlockSpec((tm, kdim), lambda i, j: (i, 0)),
                  pl.BlockSpec((tm, kdim), lambda i, j: (i, 0)),
                  pl.BlockSpec((kdim, tn), lambda i, j: (0, j)),
                  pl.BlockSpec((kdim, tn), lambda i, j: (0, j)),
                  pl.BlockSpec((tm, tn), lambda i, j: (i, j)),
                  pl.BlockSpec((tm, tn), lambda i, j: (i, j + nj))],
        out_specs=pl.BlockSpec((tm, tn), lambda i, j: (i, j)),
        out_shape=jax.ShapeDtypeStruct((m, n), BF16),
        compiler_params=_cparams(("parallel", "parallel")),
        name="merge",
    )(o, z, wr, wc, gates, gates)


def _prep_kernel(*refs, d_rwkv, r_dec, r_icl, r_gate_p, has_vres, tiles_per_seq):
    if has_vres:
        (p_ref, mu_ref, dup_ref, dbase_ref, iup_ref, ibase_ref, gup_ref,
         vd_ref, vu_ref, vb_ref, vf_ref,
         r_out, k_out, v_out, lw_out, a_out, g_out, carry_ref) = refs
    else:
        (p_ref, mu_ref, dup_ref, dbase_ref, iup_ref, ibase_ref, gup_ref,
         r_out, k_out, v_out, lw_out, a_out, g_out, carry_ref) = refs
    i = pl.program_id(0)
    first = (i % tiles_per_seq) == 0
    tp = p_ref.shape[0]
    carry = jnp.where(first, 0.0, carry_ref[...])

    def lerp(c0, c1):
        p = p_ref[:, c0:c1]
        prev = _shift_rows(p, carry[:, c0:c1], 1)
        return p + mu_ref[:, c0:c1] * (prev - p)

    d = d_rwkv
    o_d = 3 * d
    o_a = o_d + r_dec
    o_g = o_a + r_icl
    r_out[...] = lerp(0, d).astype(r_out.dtype)
    k_out[...] = lerp(d, 2 * d).astype(k_out.dtype)
    v = lerp(2 * d, 3 * d)
    d_lo = lerp(o_d, o_a)
    a_lo = lerp(o_a, o_g)
    g_lo = lerp(o_g, o_g + r_gate_p)
    carry_ref[...] = p_ref[tp - 8:tp, :]

    dec = dbase_ref[...] + _bdot(jnp.tanh(d_lo), dup_ref[...])
    lw_out[...] = -DECAY_SCALE * _sigmoid(dec)
    a_out[...] = _sigmoid(ibase_ref[...] + _bdot(a_lo, iup_ref[...])).astype(a_out.dtype)
    g_out[...] = _bdot(_sigmoid(g_lo), gup_ref[...]).astype(g_out.dtype)
    if has_vres:
        lo = _bdot(v, vd_ref[...])
        mix = _sigmoid(vb_ref[...] + _bdot(lo, vu_ref[...]))
        v = v + (vf_ref[...].astype(F32) - v) * mix
    v_out[...] = v.astype(v_out.dtype)


def _rwkv_prep(p, mu, dup, dbase, iup, ibase, gup, vres, v_first, *, d_rwkv, seq, tp=128):
    m, ncol = p.shape
    r_dec, r_icl, r_gate_p = dup.shape[0], iup.shape[0], gup.shape[0]
    has_vres = vres is not None
    row = lambda a: a.reshape(1, -1)
    full = lambda a: pl.BlockSpec(a.shape, lambda i: (0, 0))
    tile = pl.BlockSpec((tp, d_rwkv), lambda i: (i, 0))
    args = [p, row(mu), dup, row(dbase), iup, row(ibase), gup]
    in_specs = [pl.BlockSpec((tp, ncol), lambda i: (i, 0))] + [full(a) for a in args[1:]]
    if has_vres:
        vd, vu, vb = vres
        extra = [vd, vu, row(vb)]
        args += extra + [v_first]
        in_specs += [full(a) for a in extra] + [tile]
    kern = functools.partial(_prep_kernel, d_rwkv=d_rwkv, r_dec=r_dec, r_icl=r_icl,
                             r_gate_p=r_gate_p, has_vres=has_vres, tiles_per_seq=seq // tp)
    shape = lambda dt: jax.ShapeDtypeStruct((m, d_rwkv), dt)
    return pl.pallas_call(
        kern,
        grid=(m // tp,),
        in_specs=in_specs,
        out_specs=[tile] * 6,
        out_shape=[shape(BF16), shape(BF16), shape(BF16), shape(F32), shape(BF16), shape(BF16)],
        scratch_shapes=[pltpu.VMEM((8, ncol), F32)],
        compiler_params=_cparams(("arbitrary",)),
        name="rwkv_prep",
    )(*args)


def _split_bf16(x, parts):
    out = []
    for _ in range(parts):
        h = x.astype(BF16)
        out.append(h)
        x = x - h.astype(F32)
    return out


def _map(fn, *cols):
    return [fn(*args) for args in zip(*cols)]


class _WkvGeometry:
    def __init__(self, heads, chunk):
        self.g, self.c = heads, chunk
        self.w = heads * HEAD_DIM
        self.gc = heads * chunk
        c, w, gc = self.c, self.w, self.gc
        self.lane_head = lax.broadcasted_iota(jnp.int32, (c, w), 1) >> 6
        t_i = lax.broadcasted_iota(jnp.int32, (c, gc), 0)
        j_i = lax.broadcasted_iota(jnp.int32, (c, gc), 1)
        jm = j_i & (c - 1)
        self.col_head = j_i >> (c.bit_length() - 1)
        self.strict = jm < t_i
        self.incl = jm <= t_i
        self.diff = t_i ^ jm
        self.eye = (jm == t_i).astype(F32)
        bo_r = lax.broadcasted_iota(jnp.int32, (2 * w, w), 0)
        bo_c = lax.broadcasted_iota(jnp.int32, (2 * w, w), 1)
        self.block_ones = (((bo_r & (w - 1)) >> 6) == (bo_c >> 6)).astype(BF16)
        tri_r = lax.broadcasted_iota(jnp.int32, (c, 4 * c), 0)
        tri_c = lax.broadcasted_iota(jnp.int32, (c, 4 * c), 1)
        self.tri = (((tri_c & (c - 1)) <= tri_r) & (tri_c < 3 * c)).astype(BF16)
        sr = lax.broadcasted_iota(jnp.int32, (w, w), 0)
        sc = lax.broadcasted_iota(jnp.int32, (w, w), 1)
        self.state_mask = (sr >> 6) == (sc >> 6)

    def head_sum(self, x):
        hi, lo = _split_bf16(x, 2)
        return jnp.dot(jnp.concatenate([hi, lo], axis=1), self.block_ones,
                       preferred_element_type=F32)

    def cumsum(self, x):
        parts = _split_bf16(x, 3) + [jnp.zeros(x.shape, BF16)]
        return jnp.dot(self.tri, jnp.concatenate(parts, axis=0), preferred_element_type=F32)

    def stack(self, x):
        return jnp.concatenate([jnp.where(self.lane_head == h, x, 0.0) for h in range(self.g)],
                               axis=0)

    def bd(self, n):
        return jnp.concatenate([jnp.where(self.col_head == h, n, 0.0) for h in range(self.g)],
                               axis=0)


def _wkv_local(geo, xs, pars, res):
    c, gc, w = geo.c, geo.gc, geo.w
    r, k, v, lw, a = [list(col) for col in zip(*xs)]
    ks, ki, rb = [list(col) for col in zip(*[p[:3] for p in pars])]
    kk0 = _map(lambda k_, s_: k_ * s_, k, ks)
    ss = _map(lambda x: geo.head_sum(x * x), kk0)
    cum = _map(geo.cumsum, lw)
    yield
    kk = _map(lambda x, s_: x / jnp.maximum(jnp.sqrt(s_), L2_EPS), kk0, ss)
    kmod = _map(lambda k_, a_, ki_: k_ * (1.0 + (a_ - 1.0) * ki_), k, a, ki)
    cmid = [x[c // 2 - 1:c // 2, :] for x in cum]
    cend = [x[c - 1:c, :] for x in cum]
    r_s = _map(lambda r_, cu, cm: r_ * jnp.exp(cu - cm), r, cum, cmid)
    a_s = _map(lambda kk_, cu, lw_, cm: -kk_ * jnp.exp(cu - lw_ - cm), kk, cum, lw, cmid)
    e_inv = _map(lambda cu, cm: jnp.exp(cm - cu), cum, cmid)
    b_s = _map(lambda kk_, a_, e: kk_ * a_ * e, kk, a, e_inv)
    k_s = _map(lambda km, e: km * e, kmod, e_inv)
    t1 = _map(lambda as_, rs_, bs_, ks_: _bdot_nt(
        jnp.concatenate([as_, rs_], axis=0),
        jnp.concatenate([geo.stack(bs_), geo.stack(ks_)], axis=0)), a_s, r_s, b_s, k_s)
    yield
    a_ab = [jnp.where(geo.strict, t[0:c, 0:gc], 0.0) for t in t1]
    a_ak = [jnp.where(geo.strict, t[0:c, gc:2 * gc], 0.0) for t in t1]
    a_rb = [jnp.where(geo.incl, t[c:2 * c, 0:gc], 0.0) for t in t1]
    a_rk = [jnp.where(geo.incl, t[c:2 * c, gc:2 * gc], 0.0) for t in t1]
    a_d = [jnp.where(geo.diff < 8, x, 0.0) for x in a_ab]
    a2 = _map(lambda x: _bdot(x, geo.bd(x)), a_d)
    yo = _map(lambda ak, rk, v_: _bdot(jnp.concatenate([ak, rk], axis=0), geo.stack(v_)),
              a_ak, a_rk, v)
    yield
    x1 = [geo.eye + x for x in a_d]
    sq = _map(lambda x, p: _bdot(jnp.concatenate([x, p], axis=0), geo.bd(p)), x1, a2)
    yield
    x2 = _map(lambda x, s_: x + s_[0:c], x1, sq)
    inv = _map(lambda x, s_: x + _bdot(x, geo.bd(s_[c:2 * c])), x2, sq)
    yield
    s = 8
    while s < c:
        shift = s.bit_length() - 1
        tmp = _map(lambda x, ab: _bdot(x, geo.bd(jnp.where((geo.diff >> shift) == 1, ab, 0.0))),
                   inv, a_ab)
        yield
        inv = _map(lambda x, t: x + _bdot(t, geo.bd(x)), inv, tmp)
        yield
        s *= 2
    w12 = _map(lambda x, as_, yo_: _bdot(
        x, jnp.concatenate([geo.stack(as_), geo.stack(yo_[0:c])], axis=1)), inv, a_s, yo)
    yield
    res.update(r=r, v=v, kmod=kmod, rb=rb, r_s=r_s, b_s=b_s, k_s=k_s, a_rb=a_rb,
               o_loc=[x[c:2 * c] for x in yo], w1=[x[:, 0:w] for x in w12],
               w2=[x[:, w:2 * w] for x in w12], cmid=cmid, cend=cend)


def _wkv_carry(geo, loc, states, res):
    c = geo.c
    s_mid = _map(lambda s_, cm: s_ * jnp.exp(cm), states, loc["cmid"])
    d1 = _map(lambda w1, rs_, sm: _bdot_nt(jnp.concatenate([w1, rs_], axis=0), sm),
              loc["w1"], loc["r_s"], s_mid)
    yield
    u = _map(lambda d, w2: d[0:c] + w2, d1, loc["w2"])
    o = _map(lambda d, arb, u_, ol: d[c:2 * c] + _bdot(arb, geo.stack(u_)) + ol,
             d1, loc["a_rb"], u, loc["o_loc"])
    gm = _map(lambda u_, v_, bs_, ks_: _bdot_tn(jnp.concatenate([u_, v_], axis=0),
                                                jnp.concatenate([bs_, ks_], axis=0)),
              u, loc["v"], loc["b_s"], loc["k_s"])
    yield
    new = _map(lambda sm, g_, ce, cm: (sm + jnp.where(geo.state_mask, g_, 0.0)) * jnp.exp(ce - cm),
               s_mid, gm, loc["cend"], loc["cmid"])
    res.update(o=o, states=new)


def _wkv_finish(geo, loc, o, gates, pars, store):
    inv_n = 1.0 / HEAD_DIM
    mean = [geo.head_sum(x) * inv_n for x in o]
    bonus = _map(lambda r_, km, rb_: geo.head_sum(r_ * km * rb_), loc["r"], loc["kmod"], loc["rb"])
    yield
    dlt = _map(lambda x, m_: x - m_, o, mean)
    var = [geo.head_sum(x * x) * inv_n for x in dlt]
    yield
    for i, (d, vr, bn, v_, g_, p) in enumerate(zip(dlt, var, bonus, loc["v"], gates, pars)):
        store(i, (d * lax.rsqrt(vr + GN_EPS) * p[3] + p[4] + bn * v_) * g_)


def _run_interleaved(nchunk, make_local, make_carry, make_finish):
    active = [("local", 0, make_local(0))]
    local_done, carry_done, carry_started = set(), set(), set()
    while active:
        still = []
        for kind, ci, gen in active:
            try:
                next(gen)
                still.append((kind, ci, gen))
            except StopIteration:
                if kind == "local":
                    local_done.add(ci)
                    if ci + 1 < nchunk:
                        still.append(("local", ci + 1, make_local(ci + 1)))
                elif kind == "carry":
                    carry_done.add(ci)
                    still.append(("finish", ci, make_finish(ci)))
        for ci in range(nchunk):
            if (ci in local_done and ci not in carry_started
                    and (ci == 0 or ci - 1 in carry_done)):
                carry_started.add(ci)
                still.append(("carry", ci, make_carry(ci)))
        active = still


def _wkv_kernel(r_ref, k_ref, v_ref, lw_ref, a_ref, g_ref, ks_ref, ki_ref, rb_ref, lnw_ref,
                lnb_ref, o_ref, state_ref, *, nchunk, heads, chunk):
    geo = _WkvGeometry(heads, chunk)
    w = geo.w
    nbatch = r_ref.shape[0]
    ngroup = r_ref.shape[2] // w
    streams = [(b, gi) for b in range(nbatch) for gi in range(ngroup)]

    @pl.when(pl.program_id(1) == 0)
    def _():
        state_ref[...] = jnp.zeros_like(state_ref)

    pars = [tuple(p[:, gi * w:(gi + 1) * w] for p in (ks_ref, ki_ref, rb_ref, lnw_ref, lnb_ref))
            for _, gi in streams]
    local = [dict() for _ in range(nchunk)]
    carry = [dict() for _ in range(nchunk)]

    def tile(ref, ci, b, gi):
        return ref[b, pl.ds(ci * chunk, chunk), pl.ds(gi * w, w)].astype(F32)

    def make_local(ci):
        xs = [tuple(tile(ref, ci, b, gi) for ref in (r_ref, k_ref, v_ref, lw_ref, a_ref))
              for b, gi in streams]
        return _wkv_local(geo, xs, pars, local[ci])

    def make_carry(ci):
        states = (carry[ci - 1]["states"] if ci > 0
                  else [state_ref[i] for i in range(len(streams))])
        return _wkv_carry(geo, local[ci], states, carry[ci])

    def make_finish(ci):
        gates = [tile(g_ref, ci, b, gi) for b, gi in streams]

        def store(i, val):
            b, gi = streams[i]
            o_ref[b, pl.ds(ci * chunk, chunk), pl.ds(gi * w, w)] = val.astype(o_ref.dtype)

        return _wkv_finish(geo, local[ci], carry[ci]["o"], gates, pars, store)

    _run_interleaved(nchunk, make_local, make_carry, make_finish)
    for i, s_ in enumerate(carry[nchunk - 1]["states"]):
        state_ref[i] = s_


def _wkv(r, k, v, lw, a, gate, ks, ki, rb, lnw, lnb, *, batch, seq, heads, chunk, nchunk, ngroup):
    m, d = r.shape
    rows = nchunk * chunk
    lanes = ngroup * heads * HEAD_DIM
    tile = pl.BlockSpec((batch, rows, lanes), lambda h, c: (0, c, h))
    par = pl.BlockSpec((1, lanes), lambda h, c: (0, h))
    row = lambda x: x.reshape(1, d)
    b3 = lambda x: x.reshape(batch, seq, d)
    out = pl.pallas_call(
        functools.partial(_wkv_kernel, nchunk=nchunk, heads=heads, chunk=chunk),
        grid=(d // lanes, seq // rows),
        in_specs=[tile] * 6 + [par] * 5,
        out_specs=tile,
        out_shape=jax.ShapeDtypeStruct((batch, seq, d), BF16),
        scratch_shapes=[pltpu.VMEM((batch * ngroup, heads * HEAD_DIM, heads * HEAD_DIM), F32)],
        compiler_params=_cparams(("parallel", "arbitrary")),
        name="wkv7",
    )(b3(r), b3(k), b3(v), b3(lw), b3(a), b3(gate), row(ks), row(ki), row(rb), row(lnw), row(lnb))
    return out.reshape(m, d)


def _pad_to(x, size, axis):
    pad = size - x.shape[axis]
    if pad == 0:
        return x
    widths = [(0, 0)] * x.ndim
    widths[axis] = (0, pad)
    return jnp.pad(x, widths)


def _round_up(n, mult):
    return (n + mult - 1) // mult * mult


def _pick(n, prefs):
    for p in prefs:
        if n % p == 0:
            return p
    return n


def kernel(x, attn_norm, w_in, shift_mu, decay_up, decay_base, iclr_up, iclr_base, gate_up, k_scale, k_iclr, r_bonus, lnx_w, lnx_b, vres_down, vres_up, vres_base, w_out_rwkv, sconv_w, w_out_conv, w_o, ffn_norm, w_up, ffn_conv_w, w_down, final_norm):
    batch, seq, d_model = x.shape
    depth = w_in.shape[0]
    d_rwkv = decay_up.shape[2]
    d_conv = sconv_w.shape[2]
    d_ff = w_down.shape[1]
    r_dec, r_icl, r_gate, r_vres = decay_up.shape[1], iclr_up.shape[1], gate_up.shape[1], vres_down.shape[2]
    rwkv_cols = 3 * d_rwkv + r_dec + r_icl + r_gate
    conv_cols = 3 * d_conv
    m = batch * seq

    r_gate_p = _round_up(r_gate, LANES)
    r_vres_p = _round_up(r_vres, LANES)
    rwkv_cols_p = 3 * d_rwkv + r_dec + r_icl + r_gate_p

    tm = _pick(seq, (1024, 512, 256, 128))
    tn_r = _pick(rwkv_cols_p, (768, 1152, 1024, 512, 256, 128))
    tn_g = _pick(2 * d_model, (1024, 512, 256, 128))
    tn_d = _pick(d_model, (1024, 512, 256, 128))
    tn_o = _pick(d_model, (512, 256, 128))
    tc_conv = _pick(d_conv, (512, 256, 128))
    tc_ff = _pick(d_ff, (512, 256, 128))
    tk_down = _pick(d_ff, (2048, 1024, 512, 256, 128))
    tp = _pick(seq, (128,))
    wkv_nchunk = _pick(seq // WKV_CHUNK, (4, 2, 1))
    wkv_ngroup = _pick(d_rwkv // (WKV_HEADS * HEAD_DIM), (2, 1))

    xf = x.reshape(m, d_model)
    h, ssq = _prenorm(xf, attn_norm[0])
    res_dtypes = (F32, BF16, "ssq")
    v_first = None
    for l in range(depth):
        wl = w_in[l]
        w_r = _pad_to(wl[:, :rwkv_cols], rwkv_cols_p, 1).astype(BF16)
        w_c = wl[:, rwkv_cols:rwkv_cols + conv_cols].astype(BF16)
        w_g = wl[:, rwkv_cols + conv_cols:].astype(BF16)

        p_r = _matmul(h, w_r, F32, tm=tm, tn=tn_r, ssq=ssq, name="proj_rwkv")
        vres = None
        if l > 0:
            vres = (_pad_to(vres_down[l - 1], r_vres_p, 1).astype(BF16),
                    _pad_to(vres_up[l - 1], r_vres_p, 0).astype(BF16), vres_base[l - 1])
        r, k, v, lw, a, gate = _rwkv_prep(
            p_r, _pad_to(shift_mu[l], rwkv_cols_p, 0), decay_up[l].astype(BF16), decay_base[l],
            iclr_up[l].astype(BF16), iclr_base[l], _pad_to(gate_up[l], r_gate_p, 0).astype(BF16),
            vres, v_first, d_rwkv=d_rwkv, seq=seq, tp=tp)
        if l == 0:
            v_first = v
        o_rwkv = _wkv(r, k, v, lw, a, gate, k_scale[l], k_iclr[l], r_bonus[l].reshape(-1),
                      lnx_w[l], lnx_b[l], batch=batch, seq=seq, heads=WKV_HEADS, chunk=WKV_CHUNK,
                      nchunk=wkv_nchunk, ngroup=wkv_ngroup)

        z_conv = _matmul_conv(h, ssq, w_c, sconv_w[l], nparts=3, tc=tc_conv, tm=tm, seq=seq,
                              name="proj_conv")

        gates = _matmul(h, w_g, BF16, tm=tm, tn=tn_g, ssq=ssq, epilogue=_sigmoid,
                        name="proj_gates")
        merged = _merge(o_rwkv, z_conv, w_out_rwkv[l].astype(BF16), w_out_conv[l].astype(BF16),
                        gates, tm=tm, tn=tn_d)
        xf, h, ssq = _matmul(merged, w_o[l].astype(BF16), res_dtypes, tm=tm, tn=tn_o,
                             tile_extras=(xf,), row_extras=(ffn_norm[l],),
                             epilogue=_residual_epilogue, name="w_o")

        act = _matmul_conv(h, ssq, w_up[l].astype(BF16), ffn_conv_w[l], nparts=2, tc=tc_ff,
                           tm=tm, seq=seq, name="ffn_up")
        if l + 1 < depth:
            xf, h, ssq = _matmul(act, w_down[l].astype(BF16), res_dtypes, tm=tm, tn=tn_d,
                                 tk=tk_down, tile_extras=(xf,), row_extras=(attn_norm[l + 1],),
                                 epilogue=_residual_epilogue, name="w_down")
        else:
            xf = _matmul(act, w_down[l].astype(BF16), F32, tm=tm, tn=tn_d, tk=tk_down,
                         tile_extras=(xf,), epilogue=lambda acc, res: res + acc, name="w_down_last")
    out = _rmsnorm(xf, final_norm, F32)
    return out.reshape(batch, seq, d_model)
```

```python
import functools

import jax
import jax.numpy as jnp
from jax import lax
from jax.experimental import pallas as pl
from jax.experimental.pallas import tpu as pltpu

F32 = jnp.float32
BF16 = jnp.bfloat16

HEAD_DIM = 64
LANES = 128
WKV_HEADS = 4
WKV_CHUNK = 64
LOCAL_TASKS = 2
NORM_EPS = 1e-6
GN_EPS = 64e-5
DECAY_SCALE = 0.606531
L2_EPS = 1e-12
VMEM_LIMIT = 56 * 1024 * 1024


def _cparams(sem):
    return pltpu.CompilerParams(dimension_semantics=sem, vmem_limit_bytes=VMEM_LIMIT)


def _sigmoid(x):
    return 1.0 / (1.0 + jnp.exp(-x))


def _bdot(a, b):
    return jnp.dot(a.astype(BF16), b.astype(BF16), preferred_element_type=F32)


def _bdot_nt(a, b):
    return lax.dot_general(a.astype(BF16), b.astype(BF16), (((1,), (1,)), ((), ())),
                           preferred_element_type=F32)


def _bdot_tn(a, b):
    return lax.dot_general(a.astype(BF16), b.astype(BF16), (((0,), (0,)), ((), ())),
                           preferred_element_type=F32)


def _rmsnorm_kernel(x_ref, g_ref, o_ref):
    x = x_ref[...]
    ms = jnp.mean(x * x, axis=-1, keepdims=True)
    o_ref[...] = (x * lax.rsqrt(ms + NORM_EPS) * g_ref[...]).astype(o_ref.dtype)


def _rmsnorm(x, g, out_dtype, tm=256):
    m, d = x.shape
    return pl.pallas_call(
        _rmsnorm_kernel,
        grid=(m // tm,),
        in_specs=[pl.BlockSpec((tm, d), lambda i: (i, 0)),
                  pl.BlockSpec((1, d), lambda i: (0, 0))],
        out_specs=pl.BlockSpec((tm, d), lambda i: (i, 0)),
        out_shape=jax.ShapeDtypeStruct((m, d), out_dtype),
        compiler_params=_cparams(("parallel",)),
        name="rmsnorm",
    )(x, g.reshape(1, d))


def _row_scale(ssq, inv_d):
    return lax.rsqrt(jnp.sum(ssq, axis=-1, keepdims=True) * inv_d + NORM_EPS)


def _lane_partial_ssq(x):
    sq = x * x
    out = sq[:, 0:LANES]
    for c in range(1, x.shape[1] // LANES):
        out = out + sq[:, c * LANES:(c + 1) * LANES]
    return out


def _residual_epilogue(acc, res, gain):
    x = res + acc
    return x, x * gain, _lane_partial_ssq(x)


def _prenorm_kernel(x_ref, g_ref, xg_ref, ssq_ref):
    x = x_ref[...]
    xg_ref[...] = (x * g_ref[...]).astype(xg_ref.dtype)
    ssq_ref[...] = _lane_partial_ssq(x)


def _prenorm(x, g, tm=256):
    m, d = x.shape
    return pl.pallas_call(
        _prenorm_kernel,
        grid=(m // tm,),
        in_specs=[pl.BlockSpec((tm, d), lambda i: (i, 0)),
                  pl.BlockSpec((1, d), lambda i: (0, 0))],
        out_specs=[pl.BlockSpec((tm, d), lambda i: (i, 0)),
                   pl.BlockSpec((tm, LANES), lambda i: (i, 0))],
        out_shape=[jax.ShapeDtypeStruct((m, d), BF16), jax.ShapeDtypeStruct((m, LANES), F32)],
        compiler_params=_cparams(("parallel",)),
        name="prenorm",
    )(x, g.reshape(1, d))


def _accumulate_over_columns(o_ref, val):
    j = pl.program_id(1)

    @pl.when(j == 0)
    def _():
        o_ref[...] = val

    @pl.when(j != 0)
    def _():
        o_ref[...] += val


def _mm_kernel(a_ref, w_ref, *rest, epilogue, n_tile, n_row, acc_outs, inv_d):
    n_out = len(acc_outs)
    n_ssq = 0 if inv_d is None else 1
    ssq_ref = rest[0] if n_ssq else None
    tiles = rest[n_ssq:n_ssq + n_tile]
    rows = rest[n_ssq + n_tile:n_ssq + n_tile + n_row]
    outs = rest[n_ssq + n_tile + n_row:n_ssq + n_tile + n_row + n_out]
    acc = jnp.dot(a_ref[...], w_ref[...], preferred_element_type=F32)
    if n_ssq:
        acc = acc * _row_scale(ssq_ref[...], inv_d)
    vals = epilogue(acc, *[e[...] for e in tiles], *[e[...] for e in rows])
    vals = vals if isinstance(vals, tuple) else (vals,)
    for o_ref, val, accumulate in zip(outs, vals, acc_outs):
        if accumulate:
            _accumulate_over_columns(o_ref, val)
        else:
            o_ref[...] = val.astype(o_ref.dtype)


def _matmul(a, w, layer, out_dtypes, *, tm, tn, col0=0, n=None, ssq=None, tile_extras=(),
            row_extras=(), epilogue=None, name="mm"):
    m, kdim = a.shape
    n = w.shape[2] - col0 if n is None else n
    nj = n // tn
    jb = col0 // tn
    assert jb * tn == col0 and nj * tn == n
    if epilogue is None:
        epilogue = lambda acc: acc
    single = not isinstance(out_dtypes, (tuple, list))
    out_dtypes = (out_dtypes,) if single else tuple(out_dtypes)
    inv_d = None if ssq is None else 1.0 / kdim
    acc_outs = tuple(dt == "ssq" for dt in out_dtypes)
    kern = functools.partial(_mm_kernel, epilogue=epilogue, n_tile=len(tile_extras),
                             n_row=len(row_extras), acc_outs=acc_outs, inv_d=inv_d)
    in_specs = [pl.BlockSpec((tm, kdim), lambda i, j: (i, 0)),
                pl.BlockSpec((None, kdim, tn), lambda i, j: (layer, 0, j + jb))]
    args = [a, w]
    if ssq is not None:
        in_specs.append(pl.BlockSpec((tm, ssq.shape[1]), lambda i, j: (i, 0)))
        args.append(ssq)
    in_specs += [pl.BlockSpec((tm, tn), lambda i, j: (i, j)) for _ in tile_extras]
    in_specs += [pl.BlockSpec((1, tn), lambda i, j: (0, j)) for _ in row_extras]
    args += list(tile_extras) + [r.reshape(1, n) for r in row_extras]
    out_specs, out_shape = [], []
    for dt in out_dtypes:
        if dt == "ssq":
            out_specs.append(pl.BlockSpec((tm, LANES), lambda i, j: (i, 0)))
            out_shape.append(jax.ShapeDtypeStruct((m, LANES), F32))
        else:
            out_specs.append(pl.BlockSpec((tm, tn), lambda i, j: (i, j)))
            out_shape.append(jax.ShapeDtypeStruct((m, n), dt))
    col_sem = "arbitrary" if any(acc_outs) else "parallel"
    outs = pl.pallas_call(
        kern,
        grid=(m // tm, nj),
        in_specs=in_specs,
        out_specs=out_specs,
        out_shape=out_shape,
        compiler_params=_cparams(("parallel", col_sem)),
        name=name,
    )(*args)
    return outs[0] if single else outs


def _shift_rows(x, carry, nshift):
    rows = lax.broadcasted_iota(jnp.int32, x.shape, 0)
    out = pltpu.roll(x, nshift, 0)
    for r in range(nshift):
        src = carry[8 - nshift + r:8 - nshift + r + 1, :]
        out = jnp.where(rows == r, src, out)
    return out


def _conv3(x, carry, w_ref):
    x1 = _shift_rows(x, carry, 1)
    x2 = _shift_rows(x, carry, 2)
    return w_ref[0:1, :] * x2 + w_ref[1:2, :] * x1 + w_ref[2:3, :] * x


def _mm_conv_kernel(*refs, nparts, tiles_per_seq, inv_d):
    a_ref, ssq_ref = refs[0], refs[1]
    refs = refs[2:]
    w_refs = refs[0:nparts]
    ncw = 1 if nparts == 3 else 2
    cw_refs = refs[nparts:nparts + ncw]
    o_ref = refs[nparts + ncw]
    carry_refs = refs[1 + nparts + ncw:]
    i = pl.program_id(1)
    first = (i % tiles_per_seq) == 0
    a = a_ref[...]
    scale = _row_scale(ssq_ref[...], inv_d)
    accs = [jnp.dot(a, w[...], preferred_element_type=F32) * scale for w in w_refs]
    tm = accs[0].shape[0]
    if nparts == 3:
        pres = [accs[1] * accs[2]]
    else:
        pres = accs
    ys = []
    for pre, cw_ref, carry_ref in zip(pres, cw_refs, carry_refs):
        carry = jnp.where(first, 0.0, carry_ref[...])
        ys.append(_conv3(pre, carry, cw_ref))
        carry_ref[...] = pre[tm - 8:tm, :]
    if nparts == 3:
        out = accs[0] * ys[0]
    else:
        out = ys[0] * _sigmoid(ys[0]) * ys[1]
    o_ref[...] = out.astype(o_ref.dtype)


def _matmul_conv(a, ssq, w, layer, cw, *, nparts, n, tc, tm, seq, col0=0, name):
    m, kdim = a.shape
    nj = n // tc
    jb = col0 // tc
    assert jb * tc == col0 and nj * tc == n
    ncw = 1 if nparts == 3 else 2
    kern = functools.partial(_mm_conv_kernel, nparts=nparts, tiles_per_seq=seq // tm,
                             inv_d=1.0 / kdim)
    w_specs = [pl.BlockSpec((None, kdim, tc), lambda j, i, p=p: (layer, 0, jb + j + p * nj))
               for p in range(nparts)]
    cw_specs = [pl.BlockSpec((3, tc), lambda j, i, p=p: (0, j + p * nj)) for p in range(ncw)]
    return pl.pallas_call(
        kern,
        grid=(nj, m // tm),
        in_specs=[pl.BlockSpec((tm, kdim), lambda j, i: (i, 0)),
                  pl.BlockSpec((tm, ssq.shape[1]), lambda j, i: (i, 0))] + w_specs + cw_specs,
        out_specs=pl.BlockSpec((tm, tc), lambda j, i: (i, j)),
        out_shape=jax.ShapeDtypeStruct((m, n), BF16),
        scratch_shapes=[pltpu.VMEM((8, tc), F32)] * ncw,
        compiler_params=_cparams(("arbitrary", "arbitrary")),
        name=name,
    )(a, ssq, *([w] * nparts), *([cw] * ncw))


def _merge_kernel(o_ref, z_ref, wr_ref, wc_ref, gr_ref, gc_ref, out_ref):
    yr = jnp.dot(o_ref[...], wr_ref[...], preferred_element_type=F32)
    yc = jnp.dot(z_ref[...], wc_ref[...], preferred_element_type=F32)
    out = gr_ref[...].astype(F32) * yr + gc_ref[...].astype(F32) * yc
    out_ref[...] = out.astype(out_ref.dtype)


def _merge(o, z, wr, wc, layer, gates, *, tm, tn):
    m, kdim = o.shape
    n = wr.shape[2]
    nj = n // tn
    return pl.pallas_call(
        _merge_kernel,
        grid=(m // tm, nj),
        in_specs=[pl.BlockSpec((tm, kdim), lambda i, j: (i, 0)),
                  pl.BlockSpec((tm, kdim), lambda i, j: (i, 0)),
                  pl.BlockSpec((None, kdim, tn), lambda i, j: (layer, 0, j)),
                  pl.BlockSpec((None, kdim, tn), lambda i, j: (layer, 0, j)),
                  pl.BlockSpec((tm, tn), lambda i, j: (i, j)),
                  pl.BlockSpec((tm, tn), lambda i, j: (i, j + nj))],
        out_specs=pl.BlockSpec((tm, tn), lambda i, j: (i, j)),
        out_shape=jax.ShapeDtypeStruct((m, n), BF16),
        compiler_params=_cparams(("parallel", "parallel")),
        name="merge",
    )(o, z, wr, wc, gates, gates)


def _prep_kernel(*refs, d_rwkv, r_dec, r_icl, r_gate_p, has_vres, tiles_per_seq):
    if has_vres:
        (p_ref, mu_ref, dup_ref, dbase_ref, iup_ref, ibase_ref, gup_ref,
         vd_ref, vu_ref, vb_ref, vf_ref,
         r_out, k_out, v_out, lw_out, a_out, g_out, carry_ref) = refs
    else:
        (p_ref, mu_ref, dup_ref, dbase_ref, iup_ref, ibase_ref, gup_ref,
         r_out, k_out, v_out, lw_out, a_out, g_out, carry_ref) = refs
    i = pl.program_id(0)
    first = (i % tiles_per_seq) == 0
    tp = p_ref.shape[0]
    carry = jnp.where(first, 0.0, carry_ref[...])

    def lerp(c0, c1):
        p = p_ref[:, c0:c1]
        prev = _shift_rows(p, carry[:, c0:c1], 1)
        return p + mu_ref[:, c0:c1] * (prev - p)

    d = d_rwkv
    o_d = 3 * d
    o_a = o_d + r_dec
    o_g = o_a + r_icl
    r_out[...] = lerp(0, d).astype(r_out.dtype)
    k_out[...] = lerp(d, 2 * d).astype(k_out.dtype)
    v = lerp(2 * d, 3 * d)
    d_lo = lerp(o_d, o_a)
    a_lo = lerp(o_a, o_g)
    g_lo = lerp(o_g, o_g + r_gate_p)
    carry_ref[...] = p_ref[tp - 8:tp, :]

    dec = dbase_ref[...] + _bdot(jnp.tanh(d_lo), dup_ref[...])
    lw_out[...] = -DECAY_SCALE * _sigmoid(dec)
    a_out[...] = _sigmoid(ibase_ref[...] + _bdot(a_lo, iup_ref[...])).astype(a_out.dtype)
    g_out[...] = _bdot(_sigmoid(g_lo), gup_ref[...]).astype(g_out.dtype)
    if has_vres:
        lo = _bdot(v, vd_ref[...])
        mix = _sigmoid(vb_ref[...] + _bdot(lo, vu_ref[...]))
        v = v + (vf_ref[...].astype(F32) - v) * mix
    v_out[...] = v.astype(v_out.dtype)


def _rwkv_prep(p, mu, dup, dbase, iup, ibase, gup, vres, v_first, *, d_rwkv, seq, tp=128):
    m, ncol = p.shape
    r_dec, r_icl, r_gate_p = dup.shape[0], iup.shape[0], gup.shape[0]
    has_vres = vres is not None
    row = lambda a: a.reshape(1, -1)
    full = lambda a: pl.BlockSpec(a.shape, lambda i: (0, 0))
    tile = pl.BlockSpec((tp, d_rwkv), lambda i: (i, 0))
    args = [p, row(mu), dup, row(dbase), iup, row(ibase), gup]
    in_specs = [pl.BlockSpec((tp, ncol), lambda i: (i, 0))] + [full(a) for a in args[1:]]
    if has_vres:
        vd, vu, vb = vres
        extra = [vd, vu, row(vb)]
        args += extra + [v_first]
        in_specs += [full(a) for a in extra] + [tile]
    kern = functools.partial(_prep_kernel, d_rwkv=d_rwkv, r_dec=r_dec, r_icl=r_icl,
                             r_gate_p=r_gate_p, has_vres=has_vres, tiles_per_seq=seq // tp)
    shape = lambda dt: jax.ShapeDtypeStruct((m, d_rwkv), dt)
    return pl.pallas_call(
        kern,
        grid=(m // tp,),
        in_specs=in_specs,
        out_specs=[tile] * 6,
        out_shape=[shape(BF16), shape(BF16), shape(BF16), shape(F32), shape(BF16), shape(BF16)],
        scratch_shapes=[pltpu.VMEM((8, ncol), F32)],
        compiler_params=_cparams(("arbitrary",)),
        name="rwkv_prep",
    )(*args)


def _split_bf16(x, parts):
    out = []
    for _ in range(parts):
        h = x.astype(BF16)
        out.append(h)
        x = x - h.astype(F32)
    return out


def _map(fn, *cols):
    return [fn(*args) for args in zip(*cols)]


class _WkvGeometry:
    def __init__(self, heads, chunk):
        self.g, self.c = heads, chunk
        self.w = heads * HEAD_DIM
        self.gc = heads * chunk
        c, w, gc = self.c, self.w, self.gc
        self.lane_head = lax.broadcasted_iota(jnp.int32, (c, w), 1) >> 6
        t_i = lax.broadcasted_iota(jnp.int32, (c, gc), 0)
        j_i = lax.broadcasted_iota(jnp.int32, (c, gc), 1)
        jm = j_i & (c - 1)
        self.col_head = j_i >> (c.bit_length() - 1)
        self.strict = jm < t_i
        self.incl = jm <= t_i
        self.diff = t_i ^ jm
        self.eye = (jm == t_i).astype(F32)
        tri_r = lax.broadcasted_iota(jnp.int32, (c, 4 * c), 0)
        tri_c = lax.broadcasted_iota(jnp.int32, (c, 4 * c), 1)
        self.tri = (((tri_c & (c - 1)) <= tri_r) & (tri_c < 3 * c)).astype(BF16)
        sr = lax.broadcasted_iota(jnp.int32, (w, w), 0)
        sc = lax.broadcasted_iota(jnp.int32, (w, w), 1)
        self.state_mask = (sr >> 6) == (sc >> 6)
        self.block_ones = self.state_mask.astype(BF16)

    def head_sums(self, xs):
        c = self.c
        parts = [p for x in xs for p in _split_bf16(x, 2)]
        out = jnp.dot(jnp.concatenate(parts, axis=0), self.block_ones,
                      preferred_element_type=F32)
        return [out[2 * i * c:(2 * i + 1) * c] + out[(2 * i + 1) * c:(2 * i + 2) * c]
                for i in range(len(xs))]

    def cumsum(self, x):
        parts = _split_bf16(x, 3) + [jnp.zeros(x.shape, BF16)]
        return jnp.dot(self.tri, jnp.concatenate(parts, axis=0), preferred_element_type=F32)

    def stack(self, x):
        return jnp.concatenate([jnp.where(self.lane_head == h, x, 0.0) for h in range(self.g)],
                               axis=0)

    def bd(self, n):
        return jnp.concatenate([jnp.where(self.col_head == h, n, 0.0) for h in range(self.g)],
                               axis=0)


def _wkv_local(geo, xs, pars, res):
    c, gc, w = geo.c, geo.gc, geo.w
    r, k, v, lw, a = [list(col) for col in zip(*xs)]
    ks, ki, rb = [list(col) for col in zip(*[p[:3] for p in pars])]
    kk0 = _map(lambda k_, s_: k_ * s_, k, ks)
    ss = geo.head_sums([x * x for x in kk0])
    cum = _map(geo.cumsum, lw)
    yield
    kk = _map(lambda x, s_: x / jnp.maximum(jnp.sqrt(s_), L2_EPS), kk0, ss)
    kmod = _map(lambda k_, a_, ki_: k_ * (1.0 + (a_ - 1.0) * ki_), k, a, ki)
    cmid = [x[c // 2 - 1:c // 2, :] for x in cum]
    cend = [x[c - 1:c, :] for x in cum]
    r_s = _map(lambda r_, cu, cm: r_ * jnp.exp(cu - cm), r, cum, cmid)
    a_s = _map(lambda kk_, cu, lw_, cm: -kk_ * jnp.exp(cu - lw_ - cm), kk, cum, lw, cmid)
    e_inv = _map(lambda cu, cm: jnp.exp(cm - cu), cum, cmid)
    b_s = _map(lambda kk_, a_, e: kk_ * a_ * e, kk, a, e_inv)
    k_s = _map(lambda km, e: km * e, kmod, e_inv)
    t1 = _map(lambda as_, rs_, bs_, ks_: _bdot_nt(
        jnp.concatenate([as_, rs_], axis=0),
        jnp.concatenate([geo.stack(bs_), geo.stack(ks_)], axis=0)), a_s, r_s, b_s, k_s)
    yield
    a_ab = [jnp.where(geo.strict, t[0:c, 0:gc], 0.0) for t in t1]
    a_ak = [jnp.where(geo.strict, t[0:c, gc:2 * gc], 0.0) for t in t1]
    a_rb = [jnp.where(geo.incl, t[c:2 * c, 0:gc], 0.0) for t in t1]
    a_rk = [jnp.where(geo.incl, t[c:2 * c, gc:2 * gc], 0.0) for t in t1]
    a_d = [jnp.where(geo.diff < 8, x, 0.0) for x in a_ab]
    a2 = _map(lambda x: _bdot(x, geo.bd(x)), a_d)
    yo = _map(lambda ak, rk, v_: _bdot(jnp.concatenate([ak, rk], axis=0), geo.stack(v_)),
              a_ak, a_rk, v)
    yield
    x1 = [geo.eye + x for x in a_d]
    sq = _map(lambda x, p: _bdot(jnp.concatenate([x, p], axis=0), geo.bd(p)), x1, a2)
    yield
    x2 = _map(lambda x, s_: x + s_[0:c], x1, sq)
    inv = _map(lambda x, s_: x + _bdot(x, geo.bd(s_[c:2 * c])), x2, sq)
    yield
    s = 8
    while s < c:
        shift = s.bit_length() - 1
        tmp = _map(lambda x, ab: _bdot(x, geo.bd(jnp.where((geo.diff >> shift) == 1, ab, 0.0))),
                   inv, a_ab)
        yield
        inv = _map(lambda x, t: x + _bdot(t, geo.bd(x)), inv, tmp)
        yield
        s *= 2
    w12 = _map(lambda x, as_, yo_: _bdot(
        x, jnp.concatenate([geo.stack(as_), geo.stack(yo_[0:c])], axis=1)), inv, a_s, yo)
    yield
    res.update(r=r, v=v, kmod=kmod, rb=rb, r_s=r_s, b_s=b_s, k_s=k_s, a_rb=a_rb,
               o_loc=[x[c:2 * c] for x in yo], w1=[x[:, 0:w] for x in w12],
               w2=[x[:, w:2 * w] for x in w12], cmid=cmid, cend=cend)


def _wkv_carry(geo, loc, states, res):
    c = geo.c
    s_mid = _map(lambda s_, cm: s_ * jnp.exp(cm), states, loc["cmid"])
    d1 = _map(lambda w1, rs_, sm: _bdot_nt(jnp.concatenate([w1, rs_], axis=0), sm),
              loc["w1"], loc["r_s"], s_mid)
    yield
    u = _map(lambda d, w2: d[0:c] + w2, d1, loc["w2"])
    o = _map(lambda d, arb, u_, ol: d[c:2 * c] + _bdot(arb, geo.stack(u_)) + ol,
             d1, loc["a_rb"], u, loc["o_loc"])
    gm = _map(lambda u_, v_, bs_, ks_: _bdot_tn(jnp.concatenate([u_, v_], axis=0),
                                                jnp.concatenate([bs_, ks_], axis=0)),
              u, loc["v"], loc["b_s"], loc["k_s"])
    yield
    new = _map(lambda sm, g_, ce, cm: (sm + jnp.where(geo.state_mask, g_, 0.0)) * jnp.exp(ce - cm),
               s_mid, gm, loc["cend"], loc["cmid"])
    res.update(o=o, states=new)


def _wkv_finish(geo, loc, o, gates, pars, store):
    inv_n = 1.0 / HEAD_DIM
    n = len(o)
    sums = geo.head_sums(o + _map(lambda r_, km, rb_: r_ * km * rb_, loc["r"], loc["kmod"], loc["rb"]))
    mean = [x * inv_n for x in sums[:n]]
    bonus = sums[n:]
    yield
    dlt = _map(lambda x, m_: x - m_, o, mean)
    var = [x * inv_n for x in geo.head_sums([x * x for x in dlt])]
    yield
    for i, (d, vr, bn, v_, g_, p) in enumerate(zip(dlt, var, bonus, loc["v"], gates, pars)):
        store(i, (d * lax.rsqrt(vr + GN_EPS) * p[3] + p[4] + bn * v_) * g_)


def _run_interleaved(nchunk, make_local, make_carry, make_finish):
    nlocal = min(LOCAL_TASKS, nchunk)
    active = [("local", ci, make_local(ci)) for ci in range(nlocal)]
    next_local = nlocal
    local_done, carry_done, carry_started = set(), set(), set()
    while active:
        still = []
        for kind, ci, gen in active:
            try:
                next(gen)
                still.append((kind, ci, gen))
            except StopIteration:
                if kind == "local":
                    local_done.add(ci)
                    if next_local < nchunk:
                        still.append(("local", next_local, make_local(next_local)))
                        next_local += 1
                elif kind == "carry":
                    carry_done.add(ci)
                    still.append(("finish", ci, make_finish(ci)))
        for ci in range(nchunk):
            if (ci in local_done and ci not in carry_started
                    and (ci == 0 or ci - 1 in carry_done)):
                carry_started.add(ci)
                still.append(("carry", ci, make_carry(ci)))
        active = still


def _wkv_kernel(r_ref, k_ref, v_ref, lw_ref, a_ref, g_ref, ks_ref, ki_ref, rb_ref, lnw_ref,
                lnb_ref, o_ref, state_ref, *, nchunk, heads, chunk):
    geo = _WkvGeometry(heads, chunk)
    w = geo.w
    nbatch = r_ref.shape[0]
    ngroup = r_ref.shape[2] // w
    streams = [(b, gi) for b in range(nbatch) for gi in range(ngroup)]

    @pl.when(pl.program_id(1) == 0)
    def _():
        state_ref[...] = jnp.zeros_like(state_ref)

    pars = [tuple(p[:, gi * w:(gi + 1) * w] for p in (ks_ref, ki_ref, rb_ref, lnw_ref, lnb_ref))
            for _, gi in streams]
    local = [dict() for _ in range(nchunk)]
    carry = [dict() for _ in range(nchunk)]

    def tile(ref, ci, b, gi):
        return ref[b, pl.ds(ci * chunk, chunk), pl.ds(gi * w, w)].astype(F32)

    def make_local(ci):
        xs = [tuple(tile(ref, ci, b, gi) for ref in (r_ref, k_ref, v_ref, lw_ref, a_ref))
              for b, gi in streams]
        return _wkv_local(geo, xs, pars, local[ci])

    def make_carry(ci):
        states = (carry[ci - 1]["states"] if ci > 0
                  else [state_ref[i] for i in range(len(streams))])
        return _wkv_carry(geo, local[ci], states, carry[ci])

    def make_finish(ci):
        gates = [tile(g_ref, ci, b, gi) for b, gi in streams]

        def store(i, val):
            b, gi = streams[i]
            o_ref[b, pl.ds(ci * chunk, chunk), pl.ds(gi * w, w)] = val.astype(o_ref.dtype)

        return _wkv_finish(geo, local[ci], carry[ci]["o"], gates, pars, store)

    _run_interleaved(nchunk, make_local, make_carry, make_finish)
    for i, s_ in enumerate(carry[nchunk - 1]["states"]):
        state_ref[i] = s_


def _wkv(r, k, v, lw, a, gate, ks, ki, rb, lnw, lnb, *, batch, seq, heads, chunk, nchunk, ngroup):
    m, d = r.shape
    rows = nchunk * chunk
    lanes = ngroup * heads * HEAD_DIM
    tile = pl.BlockSpec((batch, rows, lanes), lambda h, c: (0, c, h))
    par = pl.BlockSpec((1, lanes), lambda h, c: (0, h))
    row = lambda x: x.reshape(1, d)
    b3 = lambda x: x.reshape(batch, seq, d)
    out = pl.pallas_call(
        functools.partial(_wkv_kernel, nchunk=nchunk, heads=heads, chunk=chunk),
        grid=(d // lanes, seq // rows),
        in_specs=[tile] * 6 + [par] * 5,
        out_specs=tile,
        out_shape=jax.ShapeDtypeStruct((batch, seq, d), BF16),
        scratch_shapes=[pltpu.VMEM((batch * ngroup, heads * HEAD_DIM, heads * HEAD_DIM), F32)],
        compiler_params=_cparams(("parallel", "arbitrary")),
        name="wkv7",
    )(b3(r), b3(k), b3(v), b3(lw), b3(a), b3(gate), row(ks), row(ki), row(rb), row(lnw), row(lnb))
    return out.reshape(m, d)


def _pad_to(x, size, axis):
    pad = size - x.shape[axis]
    if pad == 0:
        return x
    widths = [(0, 0)] * x.ndim
    widths[axis] = (0, pad)
    return jnp.pad(x, widths)


def _round_up(n, mult):
    return (n + mult - 1) // mult * mult


def _pick(n, prefs):
    for p in prefs:
        if n % p == 0:
            return p
    return n


def kernel(x, attn_norm, w_in, shift_mu, decay_up, decay_base, iclr_up, iclr_base, gate_up, k_scale, k_iclr, r_bonus, lnx_w, lnx_b, vres_down, vres_up, vres_base, w_out_rwkv, sconv_w, w_out_conv, w_o, ffn_norm, w_up, ffn_conv_w, w_down, final_norm):
    batch, seq, d_model = x.shape
    depth = w_in.shape[0]
    d_rwkv = decay_up.shape[2]
    d_conv = sconv_w.shape[2]
    d_ff = w_down.shape[1]
    r_dec, r_icl, r_gate, r_vres = decay_up.shape[1], iclr_up.shape[1], gate_up.shape[1], vres_down.shape[2]
    rwkv_cols = 3 * d_rwkv + r_dec + r_icl + r_gate
    conv_cols = 3 * d_conv
    m = batch * seq

    r_gate_p = _round_up(r_gate, LANES)
    r_vres_p = _round_up(r_vres, LANES)
    rwkv_cols_p = 3 * d_rwkv + r_dec + r_icl + r_gate_p

    tm = _pick(seq, (1024, 512, 256, 128))
    tn_r = _pick(rwkv_cols_p, (768, 1152, 1024, 512, 256, 128))
    tn_g = _pick(2 * d_model, (1024, 512, 256, 128))
    tn_d = _pick(d_model, (1024, 512, 256, 128))
    tn_o = _pick(d_model, (512, 256, 128))
    tc_conv = _pick(d_conv, (512, 256, 128))
    tc_ff = _pick(d_ff, (512, 256, 128))
    tn_dn = _pick(d_model, (256, 128))
    tp = _pick(seq, (128,))
    wkv_nchunk = _pick(seq // WKV_CHUNK, (4, 2, 1))
    wkv_ngroup = _pick(d_rwkv // (WKV_HEADS * HEAD_DIM), (2, 1))

    xf = x.reshape(m, d_model)
    h, ssq = _prenorm(xf, attn_norm[0])
    res_dtypes = (F32, BF16, "ssq")
    v_first = None

    conv0 = _round_up(rwkv_cols_p, tc_conv)
    while (conv0 + conv_cols) % tn_g:
        conv0 += tc_conv
    gate0 = conv0 + conv_cols
    w_in_b = jnp.concatenate([_pad_to(w_in[:, :, :rwkv_cols].astype(BF16), conv0, 2),
                              w_in[:, :, rwkv_cols:].astype(BF16)], axis=2)
    w_or_b, w_oc_b, w_o_b = (w.astype(BF16) for w in (w_out_rwkv, w_out_conv, w_o))
    w_up_b, w_down_b = w_up.astype(BF16), w_down.astype(BF16)

    for l in range(depth):
        p_r = _matmul(h, w_in_b, l, F32, tm=tm, tn=tn_r, n=rwkv_cols_p, ssq=ssq, name="proj_rwkv")
        vres = None
        if l > 0:
            vres = (_pad_to(vres_down[l - 1], r_vres_p, 1).astype(BF16),
                    _pad_to(vres_up[l - 1], r_vres_p, 0).astype(BF16), vres_base[l - 1])
        r, k, v, lw, a, gate = _rwkv_prep(
            p_r, _pad_to(shift_mu[l], rwkv_cols_p, 0), decay_up[l].astype(BF16), decay_base[l],
            iclr_up[l].astype(BF16), iclr_base[l], _pad_to(gate_up[l], r_gate_p, 0).astype(BF16),
            vres, v_first, d_rwkv=d_rwkv, seq=seq, tp=tp)
        if l == 0:
            v_first = v
        o_rwkv = _wkv(r, k, v, lw, a, gate, k_scale[l], k_iclr[l], r_bonus[l].reshape(-1),
                      lnx_w[l], lnx_b[l], batch=batch, seq=seq, heads=WKV_HEADS, chunk=WKV_CHUNK,
                      nchunk=wkv_nchunk, ngroup=wkv_ngroup)

        z_conv = _matmul_conv(h, ssq, w_in_b, l, sconv_w[l], nparts=3, n=d_conv, tc=tc_conv, tm=tm,
                              seq=seq, col0=conv0, name="proj_conv")

        gates = _matmul(h, w_in_b, l, BF16, tm=tm, tn=tn_g, col0=gate0, ssq=ssq, epilogue=_sigmoid,
                        name="proj_gates")
        merged = _merge(o_rwkv, z_conv, w_or_b, w_oc_b, l, gates, tm=tm, tn=tn_d)
        xf, h, ssq = _matmul(merged, w_o_b, l, res_dtypes, tm=tm, tn=tn_o,
                             tile_extras=(xf,), row_extras=(ffn_norm[l],),
                             epilogue=_residual_epilogue, name="w_o")

        act = _matmul_conv(h, ssq, w_up_b, l, ffn_conv_w[l], nparts=2, n=d_ff, tc=tc_ff, tm=tm,
                           seq=seq, name="ffn_up")
        if l + 1 < depth:
            xf, h, ssq = _matmul(act, w_down_b, l, res_dtypes, tm=tm, tn=tn_dn,
                                 tile_extras=(xf,), row_extras=(attn_norm[l + 1],),
                                 epilogue=_residual_epilogue, name="w_down")
        else:
            xf = _matmul(act, w_down_b, l, F32, tm=tm, tn=tn_dn,
                         tile_extras=(xf,), epilogue=lambda acc, res: res + acc, name="w_down_last")
    out = _rmsnorm(xf, final_norm, F32)
    return out.reshape(batch, seq, d_model)
```

```python
import functools

import jax
import jax.numpy as jnp
from jax import lax
from jax.experimental import pallas as pl
from jax.experimental.pallas import tpu as pltpu

F32 = jnp.float32
BF16 = jnp.bfloat16

HEAD_DIM = 64
LANES = 128
WKV_HEADS = 4
WKV_CHUNK = 64
LOCAL_TASKS = 2
NORM_EPS = 1e-6
GN_EPS = 64e-5
DECAY_SCALE = 0.606531
L2_EPS = 1e-12
VMEM_LIMIT = 56 * 1024 * 1024


def _cparams(sem):
    return pltpu.CompilerParams(dimension_semantics=sem, vmem_limit_bytes=VMEM_LIMIT)


def _sigmoid(x):
    return 1.0 / (1.0 + jnp.exp(-x))


def _bdot(a, b):
    return jnp.dot(a.astype(BF16), b.astype(BF16), preferred_element_type=F32)


def _bdot_nt(a, b):
    return lax.dot_general(a.astype(BF16), b.astype(BF16), (((1,), (1,)), ((), ())),
                           preferred_element_type=F32)


def _bdot_tn(a, b):
    return lax.dot_general(a.astype(BF16), b.astype(BF16), (((0,), (0,)), ((), ())),
                           preferred_element_type=F32)


def _rmsnorm_kernel(x_ref, g_ref, o_ref):
    x = x_ref[...]
    ms = jnp.mean(x * x, axis=-1, keepdims=True)
    o_ref[...] = (x * lax.rsqrt(ms + NORM_EPS) * g_ref[...]).astype(o_ref.dtype)


def _rmsnorm(x, g, out_dtype, tm=256):
    m, d = x.shape
    return pl.pallas_call(
        _rmsnorm_kernel,
        grid=(m // tm,),
        in_specs=[pl.BlockSpec((tm, d), lambda i: (i, 0)),
                  pl.BlockSpec((1, d), lambda i: (0, 0))],
        out_specs=pl.BlockSpec((tm, d), lambda i: (i, 0)),
        out_shape=jax.ShapeDtypeStruct((m, d), out_dtype),
        compiler_params=_cparams(("parallel",)),
        name="rmsnorm",
    )(x, g.reshape(1, d))


def _row_scale(ssq, inv_d):
    return lax.rsqrt(jnp.sum(ssq, axis=-1, keepdims=True) * inv_d + NORM_EPS)


def _lane_partial_ssq(x):
    sq = x * x
    out = sq[:, 0:LANES]
    for c in range(1, x.shape[1] // LANES):
        out = out + sq[:, c * LANES:(c + 1) * LANES]
    return out


def _residual_epilogue(acc, res, gain):
    x = res + acc
    return x, x * gain, _lane_partial_ssq(x)


def _prenorm_kernel(x_ref, g_ref, xg_ref, ssq_ref):
    x = x_ref[...]
    xg_ref[...] = (x * g_ref[...]).astype(xg_ref.dtype)
    ssq_ref[...] = _lane_partial_ssq(x)


def _prenorm(x, g, tm=256):
    m, d = x.shape
    return pl.pallas_call(
        _prenorm_kernel,
        grid=(m // tm,),
        in_specs=[pl.BlockSpec((tm, d), lambda i: (i, 0)),
                  pl.BlockSpec((1, d), lambda i: (0, 0))],
        out_specs=[pl.BlockSpec((tm, d), lambda i: (i, 0)),
                   pl.BlockSpec((tm, LANES), lambda i: (i, 0))],
        out_shape=[jax.ShapeDtypeStruct((m, d), BF16), jax.ShapeDtypeStruct((m, LANES), F32)],
        compiler_params=_cparams(("parallel",)),
        name="prenorm",
    )(x, g.reshape(1, d))


def _accumulate_over_columns(o_ref, val):
    j = pl.program_id(1)

    @pl.when(j == 0)
    def _():
        o_ref[...] = val

    @pl.when(j != 0)
    def _():
        o_ref[...] += val


def _mm_kernel(a_ref, w_ref, *rest, epilogue, n_tile, n_row, acc_outs, inv_d):
    n_out = len(acc_outs)
    n_ssq = 0 if inv_d is None else 1
    ssq_ref = rest[0] if n_ssq else None
    tiles = rest[n_ssq:n_ssq + n_tile]
    rows = rest[n_ssq + n_tile:n_ssq + n_tile + n_row]
    outs = rest[n_ssq + n_tile + n_row:n_ssq + n_tile + n_row + n_out]
    acc = jnp.dot(a_ref[...], w_ref[...], preferred_element_type=F32)
    if n_ssq:
        acc = acc * _row_scale(ssq_ref[...], inv_d)
    vals = epilogue(acc, *[e[...] for e in tiles], *[e[...] for e in rows])
    vals = vals if isinstance(vals, tuple) else (vals,)
    for o_ref, val, accumulate in zip(outs, vals, acc_outs):
        if accumulate:
            _accumulate_over_columns(o_ref, val)
        else:
            o_ref[...] = val.astype(o_ref.dtype)


def _matmul(a, w, layer, out_dtypes, *, tm, tn, col0=0, n=None, ssq=None, tile_extras=(),
            row_extras=(), epilogue=None, name="mm"):
    m, kdim = a.shape
    n = w.shape[2] - col0 if n is None else n
    nj = n // tn
    jb = col0 // tn
    assert jb * tn == col0 and nj * tn == n
    if epilogue is None:
        epilogue = lambda acc: acc
    single = not isinstance(out_dtypes, (tuple, list))
    out_dtypes = (out_dtypes,) if single else tuple(out_dtypes)
    inv_d = None if ssq is None else 1.0 / kdim
    acc_outs = tuple(dt == "ssq" for dt in out_dtypes)
    kern = functools.partial(_mm_kernel, epilogue=epilogue, n_tile=len(tile_extras),
                             n_row=len(row_extras), acc_outs=acc_outs, inv_d=inv_d)
    in_specs = [pl.BlockSpec((tm, kdim), lambda i, j: (i, 0)),
                pl.BlockSpec((None, kdim, tn), lambda i, j: (layer, 0, j + jb))]
    args = [a, w]
    if ssq is not None:
        in_specs.append(pl.BlockSpec((tm, ssq.shape[1]), lambda i, j: (i, 0)))
        args.append(ssq)
    in_specs += [pl.BlockSpec((tm, tn), lambda i, j: (i, j)) for _ in tile_extras]
    in_specs += [pl.BlockSpec((1, tn), lambda i, j: (0, j)) for _ in row_extras]
    args += list(tile_extras) + [r.reshape(1, n) for r in row_extras]
    out_specs, out_shape = [], []
    for dt in out_dtypes:
        if dt == "ssq":
            out_specs.append(pl.BlockSpec((tm, LANES), lambda i, j: (i, 0)))
            out_shape.append(jax.ShapeDtypeStruct((m, LANES), F32))
        else:
            out_specs.append(pl.BlockSpec((tm, tn), lambda i, j: (i, j)))
            out_shape.append(jax.ShapeDtypeStruct((m, n), dt))
    col_sem = "arbitrary" if any(acc_outs) else "parallel"
    outs = pl.pallas_call(
        kern,
        grid=(m // tm, nj),
        in_specs=in_specs,
        out_specs=out_specs,
        out_shape=out_shape,
        compiler_params=_cparams(("parallel", col_sem)),
        name=name,
    )(*args)
    return outs[0] if single else outs


def _shift_rows(x, carry, nshift):
    rows = lax.broadcasted_iota(jnp.int32, x.shape, 0)
    out = pltpu.roll(x, nshift, 0)
    for r in range(nshift):
        src = carry[8 - nshift + r:8 - nshift + r + 1, :]
        out = jnp.where(rows == r, src, out)
    return out


def _conv3(x, carry, w_ref):
    x1 = _shift_rows(x, carry, 1)
    x2 = _shift_rows(x, carry, 2)
    return w_ref[0:1, :] * x2 + w_ref[1:2, :] * x1 + w_ref[2:3, :] * x


def _mm_conv_kernel(*refs, nparts, tiles_per_seq, inv_d):
    a_ref, ssq_ref = refs[0], refs[1]
    refs = refs[2:]
    w_refs = refs[0:nparts]
    ncw = 1 if nparts == 3 else 2
    cw_refs = refs[nparts:nparts + ncw]
    o_ref = refs[nparts + ncw]
    carry_refs = refs[1 + nparts + ncw:]
    i = pl.program_id(1)
    first = (i % tiles_per_seq) == 0
    a = a_ref[...]
    scale = _row_scale(ssq_ref[...], inv_d)
    accs = [jnp.dot(a, w[...], preferred_element_type=F32) * scale for w in w_refs]
    tm = accs[0].shape[0]
    if nparts == 3:
        pres = [accs[1] * accs[2]]
    else:
        pres = accs
    ys = []
    for pre, cw_ref, carry_ref in zip(pres, cw_refs, carry_refs):
        carry = jnp.where(first, 0.0, carry_ref[...])
        ys.append(_conv3(pre, carry, cw_ref))
        carry_ref[...] = pre[tm - 8:tm, :]
    if nparts == 3:
        out = accs[0] * ys[0]
    else:
        out = ys[0] * _sigmoid(ys[0]) * ys[1]
    o_ref[...] = out.astype(o_ref.dtype)


def _matmul_conv(a, ssq, w, layer, cw, *, nparts, n, tc, tm, seq, col0=0, name):
    m, kdim = a.shape
    nj = n // tc
    jb = col0 // tc
    assert jb * tc == col0 and nj * tc == n
    ncw = 1 if nparts == 3 else 2
    kern = functools.partial(_mm_conv_kernel, nparts=nparts, tiles_per_seq=seq // tm,
                             inv_d=1.0 / kdim)
    w_specs = [pl.BlockSpec((None, kdim, tc), lambda j, i, p=p: (layer, 0, jb + j + p * nj))
               for p in range(nparts)]
    cw_specs = [pl.BlockSpec((3, tc), lambda j, i, p=p: (0, j + p * nj)) for p in range(ncw)]
    return pl.pallas_call(
        kern,
        grid=(nj, m // tm),
        in_specs=[pl.BlockSpec((tm, kdim), lambda j, i: (i, 0)),
                  pl.BlockSpec((tm, ssq.shape[1]), lambda j, i: (i, 0))] + w_specs + cw_specs,
        out_specs=pl.BlockSpec((tm, tc), lambda j, i: (i, j)),
        out_shape=jax.ShapeDtypeStruct((m, n), BF16),
        scratch_shapes=[pltpu.VMEM((8, tc), F32)] * ncw,
        compiler_params=_cparams(("arbitrary", "arbitrary")),
        name=name,
    )(a, ssq, *([w] * nparts), *([cw] * ncw))


def _merge_kernel(o_ref, z_ref, wr_ref, wc_ref, gr_ref, gc_ref, out_ref):
    yr = jnp.dot(o_ref[...], wr_ref[...], preferred_element_type=F32)
    yc = jnp.dot(z_ref[...], wc_ref[...], preferred_element_type=F32)
    out = gr_ref[...].astype(F32) * yr + gc_ref[...].astype(F32) * yc
    out_ref[...] = out.astype(out_ref.dtype)


def _merge(o, z, wr, wc, layer, gates, *, tm, tn):
    m, kdim = o.shape
    n = wr.shape[2]
    nj = n // tn
    return pl.pallas_call(
        _merge_kernel,
        grid=(m // tm, nj),
        in_specs=[pl.BlockSpec((tm, kdim), lambda i, j: (i, 0)),
                  pl.BlockSpec((tm, kdim), lambda i, j: (i, 0)),
                  pl.BlockSpec((None, kdim, tn), lambda i, j: (layer, 0, j)),
                  pl.BlockSpec((None, kdim, tn), lambda i, j: (layer, 0, j)),
                  pl.BlockSpec((tm, tn), lambda i, j: (i, j)),
                  pl.BlockSpec((tm, tn), lambda i, j: (i, j + nj))],
        out_specs=pl.BlockSpec((tm, tn), lambda i, j: (i, j)),
        out_shape=jax.ShapeDtypeStruct((m, n), BF16),
        compiler_params=_cparams(("parallel", "parallel")),
        name="merge",
    )(o, z, wr, wc, gates, gates)


def _prep_kernel(*refs, d_rwkv, r_dec, r_icl, r_gate_p, has_vres, tiles_per_seq):
    if has_vres:
        (p_ref, mu_ref, dup_ref, dbase_ref, iup_ref, ibase_ref, gup_ref,
         vd_ref, vu_ref, vb_ref, vf_ref,
         r_out, k_out, v_out, lw_out, a_out, g_out, carry_ref) = refs
    else:
        (p_ref, mu_ref, dup_ref, dbase_ref, iup_ref, ibase_ref, gup_ref,
         r_out, k_out, v_out, lw_out, a_out, g_out, carry_ref) = refs
    i = pl.program_id(0)
    first = (i % tiles_per_seq) == 0
    tp = p_ref.shape[0]
    carry = jnp.where(first, 0.0, carry_ref[...])

    def lerp(c0, c1):
        p = p_ref[:, c0:c1].astype(F32)
        prev = _shift_rows(p, carry[:, c0:c1], 1)
        return p + mu_ref[:, c0:c1] * (prev - p)

    d = d_rwkv
    o_d = 3 * d
    o_a = o_d + r_dec
    o_g = o_a + r_icl
    r_out[...] = lerp(0, d).astype(r_out.dtype)
    k_out[...] = lerp(d, 2 * d).astype(k_out.dtype)
    v = lerp(2 * d, 3 * d)
    d_lo = lerp(o_d, o_a)
    a_lo = lerp(o_a, o_g)
    g_lo = lerp(o_g, o_g + r_gate_p)
    carry_ref[...] = p_ref[tp - 16:tp, :].astype(F32)[8:16, :]

    dec = dbase_ref[...] + _bdot(jnp.tanh(d_lo), dup_ref[...])
    lw_out[...] = -DECAY_SCALE * _sigmoid(dec)
    a_out[...] = _sigmoid(ibase_ref[...] + _bdot(a_lo, iup_ref[...])).astype(a_out.dtype)
    g_out[...] = _bdot(_sigmoid(g_lo), gup_ref[...]).astype(g_out.dtype)
    if has_vres:
        lo = _bdot(v, vd_ref[...])
        mix = _sigmoid(vb_ref[...] + _bdot(lo, vu_ref[...]))
        v = v + (vf_ref[...].astype(F32) - v) * mix
    v_out[...] = v.astype(v_out.dtype)


def _rwkv_prep(p, mu, dup, dbase, iup, ibase, gup, vres, v_first, *, d_rwkv, seq, tp=128):
    m, ncol = p.shape
    r_dec, r_icl, r_gate_p = dup.shape[0], iup.shape[0], gup.shape[0]
    has_vres = vres is not None
    row = lambda a: a.reshape(1, -1)
    full = lambda a: pl.BlockSpec(a.shape, lambda i: (0, 0))
    tile = pl.BlockSpec((tp, d_rwkv), lambda i: (i, 0))
    args = [p, row(mu), dup, row(dbase), iup, row(ibase), gup]
    in_specs = [pl.BlockSpec((tp, ncol), lambda i: (i, 0))] + [full(a) for a in args[1:]]
    if has_vres:
        vd, vu, vb = vres
        extra = [vd, vu, row(vb)]
        args += extra + [v_first]
        in_specs += [full(a) for a in extra] + [tile]
    kern = functools.partial(_prep_kernel, d_rwkv=d_rwkv, r_dec=r_dec, r_icl=r_icl,
                             r_gate_p=r_gate_p, has_vres=has_vres, tiles_per_seq=seq // tp)
    shape = lambda dt: jax.ShapeDtypeStruct((m, d_rwkv), dt)
    return pl.pallas_call(
        kern,
        grid=(m // tp,),
        in_specs=in_specs,
        out_specs=[tile] * 6,
        out_shape=[shape(BF16), shape(BF16), shape(BF16), shape(F32), shape(BF16), shape(BF16)],
        scratch_shapes=[pltpu.VMEM((8, ncol), F32)],
        compiler_params=_cparams(("arbitrary",)),
        name="rwkv_prep",
    )(*args)


def _split_bf16(x, parts):
    out = []
    for _ in range(parts):
        h = x.astype(BF16)
        out.append(h)
        x = x - h.astype(F32)
    return out


def _map(fn, *cols):
    return [fn(*args) for args in zip(*cols)]


class _WkvGeometry:
    def __init__(self, heads, chunk):
        self.g, self.c = heads, chunk
        self.w = heads * HEAD_DIM
        self.gc = heads * chunk
        c, w, gc = self.c, self.w, self.gc
        self.lane_head = lax.broadcasted_iota(jnp.int32, (c, w), 1) >> 6
        t_i = lax.broadcasted_iota(jnp.int32, (c, gc), 0)
        j_i = lax.broadcasted_iota(jnp.int32, (c, gc), 1)
        jm = j_i & (c - 1)
        self.col_head = j_i >> (c.bit_length() - 1)
        self.strict = jm < t_i
        self.incl = jm <= t_i
        self.diff = t_i ^ jm
        self.eye = (jm == t_i).astype(F32)
        tri_r = lax.broadcasted_iota(jnp.int32, (c, 4 * c), 0)
        tri_c = lax.broadcasted_iota(jnp.int32, (c, 4 * c), 1)
        self.tri = (((tri_c & (c - 1)) <= tri_r) & (tri_c < 3 * c)).astype(BF16)
        sr = lax.broadcasted_iota(jnp.int32, (w, w), 0)
        sc = lax.broadcasted_iota(jnp.int32, (w, w), 1)
        self.state_mask = (sr >> 6) == (sc >> 6)
        self.block_ones = self.state_mask.astype(BF16)

    def head_sums(self, xs):
        c = self.c
        parts = [p for x in xs for p in _split_bf16(x, 2)]
        out = jnp.dot(jnp.concatenate(parts, axis=0), self.block_ones,
                      preferred_element_type=F32)
        return [out[2 * i * c:(2 * i + 1) * c] + out[(2 * i + 1) * c:(2 * i + 2) * c]
                for i in range(len(xs))]

    def cumsum(self, x):
        parts = _split_bf16(x, 3) + [jnp.zeros(x.shape, BF16)]
        return jnp.dot(self.tri, jnp.concatenate(parts, axis=0), preferred_element_type=F32)

    def stack(self, x):
        return jnp.concatenate([jnp.where(self.lane_head == h, x, 0.0) for h in range(self.g)],
                               axis=0)

    def bd(self, n):
        return jnp.concatenate([jnp.where(self.col_head == h, n, 0.0) for h in range(self.g)],
                               axis=0)


def _wkv_local(geo, xs, pars, res):
    c, gc, w = geo.c, geo.gc, geo.w
    r, k, v, lw, a = [list(col) for col in zip(*xs)]
    ks, ki, rb = [list(col) for col in zip(*[p[:3] for p in pars])]
    kk0 = _map(lambda k_, s_: k_ * s_, k, ks)
    ss = geo.head_sums([x * x for x in kk0])
    cum = _map(geo.cumsum, lw)
    yield
    kk = _map(lambda x, s_: x / jnp.maximum(jnp.sqrt(s_), L2_EPS), kk0, ss)
    kmod = _map(lambda k_, a_, ki_: k_ * (1.0 + (a_ - 1.0) * ki_), k, a, ki)
    cmid = [x[c // 2 - 1:c // 2, :] for x in cum]
    cend = [x[c - 1:c, :] for x in cum]
    r_s = _map(lambda r_, cu, cm: r_ * jnp.exp(cu - cm), r, cum, cmid)
    a_s = _map(lambda kk_, cu, lw_, cm: -kk_ * jnp.exp(cu - lw_ - cm), kk, cum, lw, cmid)
    e_inv = _map(lambda cu, cm: jnp.exp(cm - cu), cum, cmid)
    b_s = _map(lambda kk_, a_, e: kk_ * a_ * e, kk, a, e_inv)
    k_s = _map(lambda km, e: km * e, kmod, e_inv)
    t1 = _map(lambda as_, rs_, bs_, ks_: _bdot_nt(
        jnp.concatenate([as_, rs_], axis=0),
        jnp.concatenate([geo.stack(bs_), geo.stack(ks_)], axis=0)), a_s, r_s, b_s, k_s)
    yield
    a_ab = [jnp.where(geo.strict, t[0:c, 0:gc], 0.0) for t in t1]
    a_ak = [jnp.where(geo.strict, t[0:c, gc:2 * gc], 0.0) for t in t1]
    a_rb = [jnp.where(geo.incl, t[c:2 * c, 0:gc], 0.0) for t in t1]
    a_rk = [jnp.where(geo.incl, t[c:2 * c, gc:2 * gc], 0.0) for t in t1]
    a_d = [jnp.where(geo.diff < 8, x, 0.0) for x in a_ab]
    a2 = _map(lambda x: _bdot(x, geo.bd(x)), a_d)
    yo = _map(lambda ak, rk, v_: _bdot(jnp.concatenate([ak, rk], axis=0), geo.stack(v_)),
              a_ak, a_rk, v)
    yield
    x1 = [geo.eye + x for x in a_d]
    sq = _map(lambda x, p: _bdot(jnp.concatenate([x, p], axis=0), geo.bd(p)), x1, a2)
    yield
    x2 = _map(lambda x, s_: x + s_[0:c], x1, sq)
    inv = _map(lambda x, s_: x + _bdot(x, geo.bd(s_[c:2 * c])), x2, sq)
    yield
    s = 8
    while s < c:
        shift = s.bit_length() - 1
        tmp = _map(lambda x, ab: _bdot(x, geo.bd(jnp.where((geo.diff >> shift) == 1, ab, 0.0))),
                   inv, a_ab)
        yield
        inv = _map(lambda x, t: x + _bdot(t, geo.bd(x)), inv, tmp)
        yield
        s *= 2
    w12 = _map(lambda x, as_, yo_: _bdot(
        x, jnp.concatenate([geo.stack(as_), geo.stack(yo_[0:c])], axis=1)), inv, a_s, yo)
    yield
    res.update(r=r, v=v, kmod=kmod, rb=rb, r_s=r_s, b_s=b_s, k_s=k_s, a_rb=a_rb,
               o_loc=[x[c:2 * c] for x in yo], w1=[x[:, 0:w] for x in w12],
               w2=[x[:, w:2 * w] for x in w12], cmid=cmid, cend=cend)


def _wkv_carry(geo, loc, states, res):
    c = geo.c
    s_mid = _map(lambda s_, cm: s_ * jnp.exp(cm), states, loc["cmid"])
    d1 = _map(lambda w1, rs_, sm: _bdot_nt(jnp.concatenate([w1, rs_], axis=0), sm),
              loc["w1"], loc["r_s"], s_mid)
    yield
    u = _map(lambda d, w2: d[0:c] + w2, d1, loc["w2"])
    o = _map(lambda d, arb, u_, ol: d[c:2 * c] + _bdot(arb, geo.stack(u_)) + ol,
             d1, loc["a_rb"], u, loc["o_loc"])
    gm = _map(lambda u_, v_, bs_, ks_: _bdot_tn(jnp.concatenate([u_, v_], axis=0),
                                                jnp.concatenate([bs_, ks_], axis=0)),
              u, loc["v"], loc["b_s"], loc["k_s"])
    yield
    new = _map(lambda sm, g_, ce, cm: (sm + jnp.where(geo.state_mask, g_, 0.0)) * jnp.exp(ce - cm),
               s_mid, gm, loc["cend"], loc["cmid"])
    res.update(o=o, states=new)


def _wkv_finish(geo, loc, o, gates, pars, store):
    inv_n = 1.0 / HEAD_DIM
    n = len(o)
    sums = geo.head_sums(o + _map(lambda r_, km, rb_: r_ * km * rb_, loc["r"], loc["kmod"], loc["rb"]))
    mean = [x * inv_n for x in sums[:n]]
    bonus = sums[n:]
    yield
    dlt = _map(lambda x, m_: x - m_, o, mean)
    var = [x * inv_n for x in geo.head_sums([x * x for x in dlt])]
    yield
    for i, (d, vr, bn, v_, g_, p) in enumerate(zip(dlt, var, bonus, loc["v"], gates, pars)):
        store(i, (d * lax.rsqrt(vr + GN_EPS) * p[3] + p[4] + bn * v_) * g_)


def _run_interleaved(nchunk, make_local, make_carry, make_finish):
    nlocal = min(LOCAL_TASKS, nchunk)
    active = [("local", ci, make_local(ci)) for ci in range(nlocal)]
    next_local = nlocal
    local_done, carry_done, carry_started = set(), set(), set()
    while active:
        still = []
        for kind, ci, gen in active:
            try:
                next(gen)
                still.append((kind, ci, gen))
            except StopIteration:
                if kind == "local":
                    local_done.add(ci)
                    if next_local < nchunk:
                        still.append(("local", next_local, make_local(next_local)))
                        next_local += 1
                elif kind == "carry":
                    carry_done.add(ci)
                    still.append(("finish", ci, make_finish(ci)))
        for ci in range(nchunk):
            if (ci in local_done and ci not in carry_started
                    and (ci == 0 or ci - 1 in carry_done)):
                carry_started.add(ci)
                still.append(("carry", ci, make_carry(ci)))
        active = still


def _wkv_kernel(r_ref, k_ref, v_ref, lw_ref, a_ref, g_ref, ks_ref, ki_ref, rb_ref, lnw_ref,
                lnb_ref, o_ref, state_ref, *, nchunk, heads, chunk):
    geo = _WkvGeometry(heads, chunk)
    w = geo.w
    nbatch = r_ref.shape[0]
    ngroup = r_ref.shape[2] // w
    streams = [(b, gi) for b in range(nbatch) for gi in range(ngroup)]

    @pl.when(pl.program_id(1) == 0)
    def _():
        state_ref[...] = jnp.zeros_like(state_ref)

    pars = [tuple(p[:, gi * w:(gi + 1) * w] for p in (ks_ref, ki_ref, rb_ref, lnw_ref, lnb_ref))
            for _, gi in streams]
    local = [dict() for _ in range(nchunk)]
    carry = [dict() for _ in range(nchunk)]

    def tile(ref, ci, b, gi):
        return ref[b, pl.ds(ci * chunk, chunk), pl.ds(gi * w, w)].astype(F32)

    def make_local(ci):
        xs = [tuple(tile(ref, ci, b, gi) for ref in (r_ref, k_ref, v_ref, lw_ref, a_ref))
              for b, gi in streams]
        return _wkv_local(geo, xs, pars, local[ci])

    def make_carry(ci):
        states = (carry[ci - 1]["states"] if ci > 0
                  else [state_ref[i] for i in range(len(streams))])
        return _wkv_carry(geo, local[ci], states, carry[ci])

    def make_finish(ci):
        gates = [tile(g_ref, ci, b, gi) for b, gi in streams]

        def store(i, val):
            b, gi = streams[i]
            o_ref[b, pl.ds(ci * chunk, chunk), pl.ds(gi * w, w)] = val.astype(o_ref.dtype)

        return _wkv_finish(geo, local[ci], carry[ci]["o"], gates, pars, store)

    _run_interleaved(nchunk, make_local, make_carry, make_finish)
    for i, s_ in enumerate(carry[nchunk - 1]["states"]):
        state_ref[i] = s_


def _wkv(r, k, v, lw, a, gate, ks, ki, rb, lnw, lnb, *, batch, seq, heads, chunk, nchunk, ngroup):
    m, d = r.shape
    rows = nchunk * chunk
    lanes = ngroup * heads * HEAD_DIM
    tile = pl.BlockSpec((batch, rows, lanes), lambda h, c: (0, c, h))
    par = pl.BlockSpec((1, lanes), lambda h, c: (0, h))
    row = lambda x: x.reshape(1, d)
    b3 = lambda x: x.reshape(batch, seq, d)
    out = pl.pallas_call(
        functools.partial(_wkv_kernel, nchunk=nchunk, heads=heads, chunk=chunk),
        grid=(d // lanes, seq // rows),
        in_specs=[tile] * 6 + [par] * 5,
        out_specs=tile,
        out_shape=jax.ShapeDtypeStruct((batch, seq, d), BF16),
        scratch_shapes=[pltpu.VMEM((batch * ngroup, heads * HEAD_DIM, heads * HEAD_DIM), F32)],
        compiler_params=_cparams(("parallel", "arbitrary")),
        name="wkv7",
    )(b3(r), b3(k), b3(v), b3(lw), b3(a), b3(gate), row(ks), row(ki), row(rb), row(lnw), row(lnb))
    return out.reshape(m, d)


def _cast_split_kernel(x_ref, o_ref, *, split, start):
    rows, n_in = x_ref.shape
    o_ref[:, 0:split] = x_ref[:, 0:split].astype(o_ref.dtype)
    if start > split:
        o_ref[:, split:start] = jnp.zeros((rows, start - split), o_ref.dtype)
    o_ref[:, start:] = x_ref[:, split:n_in].astype(o_ref.dtype)


def _cast_split_columns(w, split, start, tk=128):
    nl, kdim, n = w.shape
    n_out = start + n - split
    return pl.pallas_call(
        functools.partial(_cast_split_kernel, split=split, start=start),
        grid=(nl, kdim // tk),
        in_specs=[pl.BlockSpec((None, tk, n), lambda l, i: (l, i, 0))],
        out_specs=pl.BlockSpec((None, tk, n_out), lambda l, i: (l, i, 0)),
        out_shape=jax.ShapeDtypeStruct((nl, kdim, n_out), BF16),
        compiler_params=_cparams(("parallel", "parallel")),
        name="cast_w_in",
    )(w)


def _pad_to(x, size, axis):
    pad = size - x.shape[axis]
    if pad == 0:
        return x
    widths = [(0, 0)] * x.ndim
    widths[axis] = (0, pad)
    return jnp.pad(x, widths)


def _round_up(n, mult):
    return (n + mult - 1) // mult * mult


def _pick(n, prefs):
    for p in prefs:
        if n % p == 0:
            return p
    return n


def kernel(x, attn_norm, w_in, shift_mu, decay_up, decay_base, iclr_up, iclr_base, gate_up, k_scale, k_iclr, r_bonus, lnx_w, lnx_b, vres_down, vres_up, vres_base, w_out_rwkv, sconv_w, w_out_conv, w_o, ffn_norm, w_up, ffn_conv_w, w_down, final_norm):
    batch, seq, d_model = x.shape
    depth = w_in.shape[0]
    d_rwkv = decay_up.shape[2]
    d_conv = sconv_w.shape[2]
    d_ff = w_down.shape[1]
    r_dec, r_icl, r_gate, r_vres = decay_up.shape[1], iclr_up.shape[1], gate_up.shape[1], vres_down.shape[2]
    rwkv_cols = 3 * d_rwkv + r_dec + r_icl + r_gate
    conv_cols = 3 * d_conv
    m = batch * seq

    r_gate_p = _round_up(r_gate, LANES)
    r_vres_p = _round_up(r_vres, LANES)
    rwkv_cols_p = 3 * d_rwkv + r_dec + r_icl + r_gate_p

    tm = _pick(seq, (1024, 512, 256, 128))
    tn_r = _pick(rwkv_cols_p, (768, 1152, 1024, 512, 256, 128))
    tn_g = _pick(2 * d_model, (1024, 512, 256, 128))
    tn_d = _pick(d_model, (1024, 512, 256, 128))
    tn_o = _pick(d_model, (512, 256, 128))
    tc_conv = _pick(d_conv, (512, 256, 128))
    tc_ff = _pick(d_ff, (512, 256, 128))
    tn_dn = _pick(d_model, (256, 128))
    tp = _pick(seq, (128,))
    wkv_nchunk = _pick(seq // WKV_CHUNK, (4, 2, 1))
    wkv_ngroup = _pick(d_rwkv // (WKV_HEADS * HEAD_DIM), (2, 1))

    xf = x.reshape(m, d_model)
    h, ssq = _prenorm(xf, attn_norm[0])
    res_dtypes = (F32, BF16, "ssq")
    v_first = None

    conv0 = _round_up(rwkv_cols_p, tc_conv)
    while (conv0 + conv_cols) % tn_g:
        conv0 += tc_conv
    gate0 = conv0 + conv_cols
    w_in_b = _cast_split_columns(w_in, rwkv_cols, conv0)
    w_or_b, w_oc_b, w_o_b = (w.astype(BF16) for w in (w_out_rwkv, w_out_conv, w_o))
    w_up_b, w_down_b = w_up.astype(BF16), w_down.astype(BF16)

    for l in range(depth):
        p_r = _matmul(h, w_in_b, l, BF16, tm=tm, tn=tn_r, n=rwkv_cols_p, ssq=ssq, name="proj_rwkv")
        vres = None
        if l > 0:
            vres = (_pad_to(vres_down[l - 1], r_vres_p, 1).astype(BF16),
                    _pad_to(vres_up[l - 1], r_vres_p, 0).astype(BF16), vres_base[l - 1])
        r, k, v, lw, a, gate = _rwkv_prep(
            p_r, _pad_to(shift_mu[l], rwkv_cols_p, 0), decay_up[l].astype(BF16), decay_base[l],
            iclr_up[l].astype(BF16), iclr_base[l], _pad_to(gate_up[l], r_gate_p, 0).astype(BF16),
            vres, v_first, d_rwkv=d_rwkv, seq=seq, tp=tp)
        if l == 0:
            v_first = v
        o_rwkv = _wkv(r, k, v, lw, a, gate, k_scale[l], k_iclr[l], r_bonus[l].reshape(-1),
                      lnx_w[l], lnx_b[l], batch=batch, seq=seq, heads=WKV_HEADS, chunk=WKV_CHUNK,
                      nchunk=wkv_nchunk, ngroup=wkv_ngroup)

        z_conv = _matmul_conv(h, ssq, w_in_b, l, sconv_w[l], nparts=3, n=d_conv, tc=tc_conv, tm=tm,
                              seq=seq, col0=conv0, name="proj_conv")

        gates = _matmul(h, w_in_b, l, BF16, tm=tm, tn=tn_g, col0=gate0, ssq=ssq, epilogue=_sigmoid,
                        name="proj_gates")
        merged = _merge(o_rwkv, z_conv, w_or_b, w_oc_b, l, gates, tm=tm, tn=tn_d)
        xf, h, ssq = _matmul(merged, w_o_b, l, res_dtypes, tm=tm, tn=tn_o,
                             tile_extras=(xf,), row_extras=(ffn_norm[l],),
                             epilogue=_residual_epilogue, name="w_o")

        act = _matmul_conv(h, ssq, w_up_b, l, ffn_conv_w[l], nparts=2, n=d_ff, tc=tc_ff, tm=tm,
                           seq=seq, name="ffn_up")
        if l + 1 < depth:
            xf, h, ssq = _matmul(act, w_down_b, l, res_dtypes, tm=tm, tn=tn_dn,
                                 tile_extras=(xf,), row_extras=(attn_norm[l + 1],),
                                 epilogue=_residual_epilogue, name="w_down")
        else:
            xf = _matmul(act, w_down_b, l, F32, tm=tm, tn=tn_dn,
                         tile_extras=(xf,), epilogue=lambda acc, res: res + acc, name="w_down_last")
    out = _rmsnorm(xf, final_norm, F32)
    return out.reshape(batch, seq, d_model)
```

```python
import functools

import jax
import jax.numpy as jnp
from jax import lax
from jax.experimental import pallas as pl
from jax.experimental.pallas import tpu as pltpu

F32 = jnp.float32
BF16 = jnp.bfloat16

HEAD_DIM = 64
LANES = 128
WKV_HEADS = 4
WKV_CHUNK = 64
LOCAL_TASKS = 2
NORM_EPS = 1e-6
GN_EPS = 64e-5
DECAY_SCALE = 0.606531
L2_EPS = 1e-12
VMEM_LIMIT = 56 * 1024 * 1024


def _cparams(sem):
    return pltpu.CompilerParams(dimension_semantics=sem, vmem_limit_bytes=VMEM_LIMIT)


def _sigmoid(x):
    return 1.0 / (1.0 + jnp.exp(-x))


def _bdot(a, b):
    return jnp.dot(a.astype(BF16), b.astype(BF16), preferred_element_type=F32)


def _bdot_nt(a, b):
    return lax.dot_general(a.astype(BF16), b.astype(BF16), (((1,), (1,)), ((), ())),
                           preferred_element_type=F32)


def _bdot_tn(a, b):
    return lax.dot_general(a.astype(BF16), b.astype(BF16), (((0,), (0,)), ((), ())),
                           preferred_element_type=F32)


def _rmsnorm_kernel(x_ref, g_ref, o_ref):
    x = x_ref[...]
    ms = jnp.mean(x * x, axis=-1, keepdims=True)
    o_ref[...] = (x * lax.rsqrt(ms + NORM_EPS) * g_ref[...]).astype(o_ref.dtype)


def _rmsnorm(x, g, out_dtype, tm=256):
    m, d = x.shape
    return pl.pallas_call(
        _rmsnorm_kernel,
        grid=(m // tm,),
        in_specs=[pl.BlockSpec((tm, d), lambda i: (i, 0)),
                  pl.BlockSpec((1, d), lambda i: (0, 0))],
        out_specs=pl.BlockSpec((tm, d), lambda i: (i, 0)),
        out_shape=jax.ShapeDtypeStruct((m, d), out_dtype),
        compiler_params=_cparams(("parallel",)),
        name="rmsnorm",
    )(x, g.reshape(1, d))


def _row_scale(ssq, inv_d):
    return lax.rsqrt(jnp.sum(ssq, axis=-1, keepdims=True) * inv_d + NORM_EPS)


def _lane_partial_ssq(x):
    sq = x * x
    out = sq[:, 0:LANES]
    for c in range(1, x.shape[1] // LANES):
        out = out + sq[:, c * LANES:(c + 1) * LANES]
    return out


def _residual_epilogue(acc, res, gain):
    x = res + acc
    return x, x * gain, _lane_partial_ssq(x)


def _prenorm_kernel(x_ref, g_ref, xg_ref, ssq_ref):
    x = x_ref[...]
    xg_ref[...] = (x * g_ref[...]).astype(xg_ref.dtype)
    ssq_ref[...] = _lane_partial_ssq(x)


def _prenorm(x, g, tm=256):
    m, d = x.shape
    return pl.pallas_call(
        _prenorm_kernel,
        grid=(m // tm,),
        in_specs=[pl.BlockSpec((tm, d), lambda i: (i, 0)),
                  pl.BlockSpec((1, d), lambda i: (0, 0))],
        out_specs=[pl.BlockSpec((tm, d), lambda i: (i, 0)),
                   pl.BlockSpec((tm, LANES), lambda i: (i, 0))],
        out_shape=[jax.ShapeDtypeStruct((m, d), BF16), jax.ShapeDtypeStruct((m, LANES), F32)],
        compiler_params=_cparams(("parallel",)),
        name="prenorm",
    )(x, g.reshape(1, d))


def _accumulate_over_columns(o_ref, val):
    j = pl.program_id(1)

    @pl.when(j == 0)
    def _():
        o_ref[...] = val

    @pl.when(j != 0)
    def _():
        o_ref[...] += val


def _mm_kernel(a_ref, w_ref, *rest, epilogue, n_tile, n_row, acc_outs, inv_d):
    n_out = len(acc_outs)
    n_ssq = 0 if inv_d is None else 1
    ssq_ref = rest[0] if n_ssq else None
    tiles = rest[n_ssq:n_ssq + n_tile]
    rows = rest[n_ssq + n_tile:n_ssq + n_tile + n_row]
    outs = rest[n_ssq + n_tile + n_row:n_ssq + n_tile + n_row + n_out]
    acc = jnp.dot(a_ref[...], w_ref[...], preferred_element_type=F32)
    if n_ssq:
        acc = acc * _row_scale(ssq_ref[...], inv_d)
    vals = epilogue(acc, *[e[...] for e in tiles], *[e[...] for e in rows])
    vals = vals if isinstance(vals, tuple) else (vals,)
    for o_ref, val, accumulate in zip(outs, vals, acc_outs):
        if accumulate:
            _accumulate_over_columns(o_ref, val)
        else:
            o_ref[...] = val.astype(o_ref.dtype)


def _matmul(a, w, layer, out_dtypes, *, tm, tn, col0=0, n=None, ssq=None, tile_extras=(),
            row_extras=(), epilogue=None, name="mm"):
    m, kdim = a.shape
    n = w.shape[2] - col0 if n is None else n
    nj = n // tn
    jb = col0 // tn
    assert jb * tn == col0 and nj * tn == n
    if epilogue is None:
        epilogue = lambda acc: acc
    single = not isinstance(out_dtypes, (tuple, list))
    out_dtypes = (out_dtypes,) if single else tuple(out_dtypes)
    inv_d = None if ssq is None else 1.0 / kdim
    acc_outs = tuple(dt == "ssq" for dt in out_dtypes)
    kern = functools.partial(_mm_kernel, epilogue=epilogue, n_tile=len(tile_extras),
                             n_row=len(row_extras), acc_outs=acc_outs, inv_d=inv_d)
    in_specs = [pl.BlockSpec((tm, kdim), lambda i, j: (i, 0)),
                pl.BlockSpec((None, kdim, tn), lambda i, j: (layer, 0, j + jb))]
    args = [a, w]
    if ssq is not None:
        in_specs.append(pl.BlockSpec((tm, ssq.shape[1]), lambda i, j: (i, 0)))
        args.append(ssq)
    in_specs += [pl.BlockSpec((tm, tn), lambda i, j: (i, j)) for _ in tile_extras]
    in_specs += [pl.BlockSpec((1, tn), lambda i, j: (0, j)) for _ in row_extras]
    args += list(tile_extras) + [r.reshape(1, n) for r in row_extras]
    out_specs, out_shape = [], []
    for dt in out_dtypes:
        if dt == "ssq":
            out_specs.append(pl.BlockSpec((tm, LANES), lambda i, j: (i, 0)))
            out_shape.append(jax.ShapeDtypeStruct((m, LANES), F32))
        else:
            out_specs.append(pl.BlockSpec((tm, tn), lambda i, j: (i, j)))
            out_shape.append(jax.ShapeDtypeStruct((m, n), dt))
    col_sem = "arbitrary" if any(acc_outs) else "parallel"
    outs = pl.pallas_call(
        kern,
        grid=(m // tm, nj),
        in_specs=in_specs,
        out_specs=out_specs,
        out_shape=out_shape,
        compiler_params=_cparams(("parallel", col_sem)),
        name=name,
    )(*args)
    return outs[0] if single else outs


def _shift_rows(x, carry, nshift):
    rows = lax.broadcasted_iota(jnp.int32, x.shape, 0)
    out = pltpu.roll(x, nshift, 0)
    for r in range(nshift):
        src = carry[8 - nshift + r:8 - nshift + r + 1, :]
        out = jnp.where(rows == r, src, out)
    return out


def _conv3(x, carry, w_ref):
    x1 = _shift_rows(x, carry, 1)
    x2 = _shift_rows(x, carry, 2)
    return w_ref[0:1, :] * x2 + w_ref[1:2, :] * x1 + w_ref[2:3, :] * x


def _mm_conv_kernel(*refs, nparts, tiles_per_seq, inv_d):
    a_ref, ssq_ref = refs[0], refs[1]
    refs = refs[2:]
    w_refs = refs[0:nparts]
    ncw = 1 if nparts == 3 else 2
    cw_refs = refs[nparts:nparts + ncw]
    o_ref = refs[nparts + ncw]
    carry_refs = refs[1 + nparts + ncw:]
    i = pl.program_id(1)
    first = (i % tiles_per_seq) == 0
    a = a_ref[...]
    scale = _row_scale(ssq_ref[...], inv_d)
    accs = [jnp.dot(a, w[...], preferred_element_type=F32) * scale for w in w_refs]
    tm = accs[0].shape[0]
    if nparts == 3:
        pres = [accs[1] * accs[2]]
    else:
        pres = accs
    ys = []
    for pre, cw_ref, carry_ref in zip(pres, cw_refs, carry_refs):
        carry = jnp.where(first, 0.0, carry_ref[...])
        ys.append(_conv3(pre, carry, cw_ref))
        carry_ref[...] = pre[tm - 8:tm, :]
    if nparts == 3:
        out = accs[0] * ys[0]
    else:
        out = ys[0] * _sigmoid(ys[0]) * ys[1]
    o_ref[...] = out.astype(o_ref.dtype)


def _matmul_conv(a, ssq, w, layer, cw, *, nparts, n, tc, tm, seq, col0=0, name):
    m, kdim = a.shape
    nj = n // tc
    jb = col0 // tc
    assert jb * tc == col0 and nj * tc == n
    ncw = 1 if nparts == 3 else 2
    kern = functools.partial(_mm_conv_kernel, nparts=nparts, tiles_per_seq=seq // tm,
                             inv_d=1.0 / kdim)
    w_specs = [pl.BlockSpec((None, kdim, tc), lambda j, i, p=p: (layer, 0, jb + j + p * nj))
               for p in range(nparts)]
    cw_specs = [pl.BlockSpec((3, tc), lambda j, i, p=p: (0, j + p * nj)) for p in range(ncw)]
    return pl.pallas_call(
        kern,
        grid=(nj, m // tm),
        in_specs=[pl.BlockSpec((tm, kdim), lambda j, i: (i, 0)),
                  pl.BlockSpec((tm, ssq.shape[1]), lambda j, i: (i, 0))] + w_specs + cw_specs,
        out_specs=pl.BlockSpec((tm, tc), lambda j, i: (i, j)),
        out_shape=jax.ShapeDtypeStruct((m, n), BF16),
        scratch_shapes=[pltpu.VMEM((8, tc), F32)] * ncw,
        compiler_params=_cparams(("arbitrary", "arbitrary")),
        name=name,
    )(a, ssq, *([w] * nparts), *([cw] * ncw))


def _merge_kernel(o_ref, z_ref, wr_ref, wc_ref, gr_ref, gc_ref, out_ref):
    yr = jnp.dot(o_ref[...], wr_ref[...], preferred_element_type=F32)
    yc = jnp.dot(z_ref[...], wc_ref[...], preferred_element_type=F32)
    out = gr_ref[...].astype(F32) * yr + gc_ref[...].astype(F32) * yc
    out_ref[...] = out.astype(out_ref.dtype)


def _merge(o, z, wr, wc, layer, gates, *, tm, tn):
    m, kdim = o.shape
    n = wr.shape[2]
    nj = n // tn
    return pl.pallas_call(
        _merge_kernel,
        grid=(m // tm, nj),
        in_specs=[pl.BlockSpec((tm, kdim), lambda i, j: (i, 0)),
                  pl.BlockSpec((tm, kdim), lambda i, j: (i, 0)),
                  pl.BlockSpec((None, kdim, tn), lambda i, j: (layer, 0, j)),
                  pl.BlockSpec((None, kdim, tn), lambda i, j: (layer, 0, j)),
                  pl.BlockSpec((tm, tn), lambda i, j: (i, j)),
                  pl.BlockSpec((tm, tn), lambda i, j: (i, j + nj))],
        out_specs=pl.BlockSpec((tm, tn), lambda i, j: (i, j)),
        out_shape=jax.ShapeDtypeStruct((m, n), BF16),
        compiler_params=_cparams(("parallel", "parallel")),
        name="merge",
    )(o, z, wr, wc, gates, gates)


def _prep_kernel(*refs, d_rwkv, r_dec, r_icl, r_gate_p, has_vres, tiles_per_seq):
    if has_vres:
        (p_ref, mu_ref, dup_ref, dbase_ref, iup_ref, ibase_ref, gup_ref,
         vd_ref, vu_ref, vb_ref, vf_ref,
         r_out, k_out, v_out, lw_out, a_out, g_out, carry_ref) = refs
    else:
        (p_ref, mu_ref, dup_ref, dbase_ref, iup_ref, ibase_ref, gup_ref,
         r_out, k_out, v_out, lw_out, a_out, g_out, carry_ref) = refs
    i = pl.program_id(0)
    first = (i % tiles_per_seq) == 0
    tp = p_ref.shape[0]
    carry = jnp.where(first, 0.0, carry_ref[...])

    def lerp(c0, c1):
        p = p_ref[:, c0:c1].astype(F32)
        prev = _shift_rows(p, carry[:, c0:c1], 1)
        return p + mu_ref[:, c0:c1] * (prev - p)

    d = d_rwkv
    o_d = 3 * d
    o_a = o_d + r_dec
    o_g = o_a + r_icl
    r_out[...] = lerp(0, d).astype(r_out.dtype)
    k_out[...] = lerp(d, 2 * d).astype(k_out.dtype)
    v = lerp(2 * d, 3 * d)
    d_lo = lerp(o_d, o_a)
    a_lo = lerp(o_a, o_g)
    g_lo = lerp(o_g, o_g + r_gate_p)
    carry_ref[...] = p_ref[tp - 16:tp, :].astype(F32)[8:16, :]

    dec = dbase_ref[...] + _bdot(jnp.tanh(d_lo), dup_ref[...])
    lw_out[...] = -DECAY_SCALE * _sigmoid(dec)
    a_out[...] = _sigmoid(ibase_ref[...] + _bdot(a_lo, iup_ref[...])).astype(a_out.dtype)
    g_out[...] = _bdot(_sigmoid(g_lo), gup_ref[...]).astype(g_out.dtype)
    if has_vres:
        lo = _bdot(v, vd_ref[...])
        mix = _sigmoid(vb_ref[...] + _bdot(lo, vu_ref[...]))
        v = v + (vf_ref[...].astype(F32) - v) * mix
    v_out[...] = v.astype(v_out.dtype)


def _rwkv_prep(p, mu, dup, dbase, iup, ibase, gup, vres, v_first, *, d_rwkv, seq, tp=128):
    m, ncol = p.shape
    r_dec, r_icl, r_gate_p = dup.shape[0], iup.shape[0], gup.shape[0]
    has_vres = vres is not None
    row = lambda a: a.reshape(1, -1)
    full = lambda a: pl.BlockSpec(a.shape, lambda i: (0, 0))
    tile = pl.BlockSpec((tp, d_rwkv), lambda i: (i, 0))
    args = [p, row(mu), dup, row(dbase), iup, row(ibase), gup]
    in_specs = [pl.BlockSpec((tp, ncol), lambda i: (i, 0))] + [full(a) for a in args[1:]]
    if has_vres:
        vd, vu, vb = vres
        extra = [vd, vu, row(vb)]
        args += extra + [v_first]
        in_specs += [full(a) for a in extra] + [tile]
    kern = functools.partial(_prep_kernel, d_rwkv=d_rwkv, r_dec=r_dec, r_icl=r_icl,
                             r_gate_p=r_gate_p, has_vres=has_vres, tiles_per_seq=seq // tp)
    shape = lambda dt: jax.ShapeDtypeStruct((m, d_rwkv), dt)
    return pl.pallas_call(
        kern,
        grid=(m // tp,),
        in_specs=in_specs,
        out_specs=[tile] * 6,
        out_shape=[shape(BF16), shape(BF16), shape(BF16), shape(F32), shape(BF16), shape(BF16)],
        scratch_shapes=[pltpu.VMEM((8, ncol), F32)],
        compiler_params=_cparams(("arbitrary",)),
        name="rwkv_prep",
    )(*args)


def _split_bf16(x, parts):
    out = []
    for _ in range(parts):
        h = x.astype(BF16)
        out.append(h)
        x = x - h.astype(F32)
    return out


def _map(fn, *cols):
    return [fn(*args) for args in zip(*cols)]


class _WkvGeometry:
    def __init__(self, heads, chunk):
        self.g, self.c = heads, chunk
        self.w = heads * HEAD_DIM
        self.gc = heads * chunk
        c, w, gc = self.c, self.w, self.gc
        self.lane_head = lax.broadcasted_iota(jnp.int32, (c, w), 1) >> 6
        t_i = lax.broadcasted_iota(jnp.int32, (c, gc), 0)
        j_i = lax.broadcasted_iota(jnp.int32, (c, gc), 1)
        jm = j_i & (c - 1)
        self.col_head = j_i >> (c.bit_length() - 1)
        self.strict = jm < t_i
        self.incl = jm <= t_i
        self.diff = t_i ^ jm
        self.eye = (jm == t_i).astype(F32)
        tri_r = lax.broadcasted_iota(jnp.int32, (c, 4 * c), 0)
        tri_c = lax.broadcasted_iota(jnp.int32, (c, 4 * c), 1)
        self.tri = (((tri_c & (c - 1)) <= tri_r) & (tri_c < 3 * c)).astype(BF16)
        sr = lax.broadcasted_iota(jnp.int32, (w, w), 0)
        sc = lax.broadcasted_iota(jnp.int32, (w, w), 1)
        self.state_mask = (sr >> 6) == (sc >> 6)
        self.block_ones = self.state_mask.astype(BF16)

    def head_sums(self, xs):
        c = self.c
        parts = [p for x in xs for p in _split_bf16(x, 2)]
        out = jnp.dot(jnp.concatenate(parts, axis=0), self.block_ones,
                      preferred_element_type=F32)
        return [out[2 * i * c:(2 * i + 1) * c] + out[(2 * i + 1) * c:(2 * i + 2) * c]
                for i in range(len(xs))]

    def cumsum(self, x):
        parts = _split_bf16(x, 3) + [jnp.zeros(x.shape, BF16)]
        return jnp.dot(self.tri, jnp.concatenate(parts, axis=0), preferred_element_type=F32)

    def stack(self, x):
        return jnp.concatenate([jnp.where(self.lane_head == h, x, 0.0) for h in range(self.g)],
                               axis=0)

    def bd(self, n):
        return jnp.concatenate([jnp.where(self.col_head == h, n, 0.0) for h in range(self.g)],
                               axis=0)


def _wkv_local(geo, xs, pars, res):
    c, gc, w = geo.c, geo.gc, geo.w
    r, k, v, lw, a = [list(col) for col in zip(*xs)]
    ks, ki, rb = [list(col) for col in zip(*[p[:3] for p in pars])]
    kk0 = _map(lambda k_, s_: k_ * s_, k, ks)
    ss = geo.head_sums([x * x for x in kk0])
    cum = _map(geo.cumsum, lw)
    yield
    kk = _map(lambda x, s_: x / jnp.maximum(jnp.sqrt(s_), L2_EPS), kk0, ss)
    kmod = _map(lambda k_, a_, ki_: k_ * (1.0 + (a_ - 1.0) * ki_), k, a, ki)
    cmid = [x[c // 2 - 1:c // 2, :] for x in cum]
    cend = [x[c - 1:c, :] for x in cum]
    r_s = _map(lambda r_, cu, cm: r_ * jnp.exp(cu - cm), r, cum, cmid)
    a_s = _map(lambda kk_, cu, lw_, cm: -kk_ * jnp.exp(cu - lw_ - cm), kk, cum, lw, cmid)
    e_inv = _map(lambda cu, cm: jnp.exp(cm - cu), cum, cmid)
    b_s = _map(lambda kk_, a_, e: kk_ * a_ * e, kk, a, e_inv)
    k_s = _map(lambda km, e: km * e, kmod, e_inv)
    t1 = _map(lambda as_, rs_, bs_, ks_: _bdot_nt(
        jnp.concatenate([as_, rs_], axis=0),
        jnp.concatenate([geo.stack(bs_), geo.stack(ks_)], axis=0)), a_s, r_s, b_s, k_s)
    yield
    a_ab = [jnp.where(geo.strict, t[0:c, 0:gc], 0.0) for t in t1]
    a_ak = [jnp.where(geo.strict, t[0:c, gc:2 * gc], 0.0) for t in t1]
    a_rb = [jnp.where(geo.incl, t[c:2 * c, 0:gc], 0.0) for t in t1]
    a_rk = [jnp.where(geo.incl, t[c:2 * c, gc:2 * gc], 0.0) for t in t1]
    a_d = [jnp.where(geo.diff < 8, x, 0.0) for x in a_ab]
    a2 = _map(lambda x: _bdot(x, geo.bd(x)), a_d)
    yo = _map(lambda ak, rk, v_: _bdot(jnp.concatenate([ak, rk], axis=0), geo.stack(v_)),
              a_ak, a_rk, v)
    yield
    x1 = [geo.eye + x for x in a_d]
    sq = _map(lambda x, p: _bdot(jnp.concatenate([x, p], axis=0), geo.bd(p)), x1, a2)
    yield
    x2 = _map(lambda x, s_: x + s_[0:c], x1, sq)
    inv = _map(lambda x, s_: x + _bdot(x, geo.bd(s_[c:2 * c])), x2, sq)
    yield
    s = 8
    while s < c:
        shift = s.bit_length() - 1
        tmp = _map(lambda x, ab: _bdot(x, geo.bd(jnp.where((geo.diff >> shift) == 1, ab, 0.0))),
                   inv, a_ab)
        yield
        inv = _map(lambda x, t: x + _bdot(t, geo.bd(x)), inv, tmp)
        yield
        s *= 2
    w12 = _map(lambda x, as_, yo_: _bdot(
        x, jnp.concatenate([geo.stack(as_), geo.stack(yo_[0:c])], axis=1)), inv, a_s, yo)
    yield
    res.update(r=r, v=v, kmod=kmod, rb=rb, r_s=r_s, b_s=b_s, k_s=k_s, a_rb=a_rb,
               o_loc=[x[c:2 * c] for x in yo], w1=[x[:, 0:w] for x in w12],
               w2=[x[:, w:2 * w] for x in w12], cmid=cmid, cend=cend)


def _wkv_carry(geo, loc, states, res):
    c = geo.c
    s_mid = _map(lambda s_, cm: s_ * jnp.exp(cm), states, loc["cmid"])
    d1 = _map(lambda w1, rs_, sm: _bdot_nt(jnp.concatenate([w1, rs_], axis=0), sm),
              loc["w1"], loc["r_s"], s_mid)
    yield
    u = _map(lambda d, w2: d[0:c] + w2, d1, loc["w2"])
    o = _map(lambda d, arb, u_, ol: d[c:2 * c] + _bdot(arb, geo.stack(u_)) + ol,
             d1, loc["a_rb"], u, loc["o_loc"])
    gm = _map(lambda u_, v_, bs_, ks_: _bdot_tn(jnp.concatenate([u_, v_], axis=0),
                                                jnp.concatenate([bs_, ks_], axis=0)),
              u, loc["v"], loc["b_s"], loc["k_s"])
    yield
    new = _map(lambda sm, g_, ce, cm: (sm + jnp.where(geo.state_mask, g_, 0.0)) * jnp.exp(ce - cm),
               s_mid, gm, loc["cend"], loc["cmid"])
    res.update(o=o, states=new)


def _wkv_finish(geo, loc, o, gates, pars, store):
    inv_n = 1.0 / HEAD_DIM
    n = len(o)
    sums = geo.head_sums(o + _map(lambda r_, km, rb_: r_ * km * rb_, loc["r"], loc["kmod"], loc["rb"]))
    mean = [x * inv_n for x in sums[:n]]
    bonus = sums[n:]
    yield
    dlt = _map(lambda x, m_: x - m_, o, mean)
    var = [x * inv_n for x in geo.head_sums([x * x for x in dlt])]
    yield
    for i, (d, vr, bn, v_, g_, p) in enumerate(zip(dlt, var, bonus, loc["v"], gates, pars)):
        store(i, (d * lax.rsqrt(vr + GN_EPS) * p[3] + p[4] + bn * v_) * g_)


def _run_interleaved(nchunk, make_local, make_carry, make_finish):
    nlocal = min(LOCAL_TASKS, nchunk)
    active = [("local", ci, make_local(ci)) for ci in range(nlocal)]
    next_local = nlocal
    local_done, carry_done, carry_started = set(), set(), set()
    while active:
        still = []
        for kind, ci, gen in active:
            try:
                next(gen)
                still.append((kind, ci, gen))
            except StopIteration:
                if kind == "local":
                    local_done.add(ci)
                    if next_local < nchunk:
                        still.append(("local", next_local, make_local(next_local)))
                        next_local += 1
                elif kind == "carry":
                    carry_done.add(ci)
                    still.append(("finish", ci, make_finish(ci)))
        for ci in range(nchunk):
            if (ci in local_done and ci not in carry_started
                    and (ci == 0 or ci - 1 in carry_done)):
                carry_started.add(ci)
                still.append(("carry", ci, make_carry(ci)))
        active = still


def _wkv_kernel(r_ref, k_ref, v_ref, lw_ref, a_ref, g_ref, ks_ref, ki_ref, rb_ref, lnw_ref,
                lnb_ref, o_ref, state_ref, *, nchunk, heads, chunk):
    geo = _WkvGeometry(heads, chunk)
    w = geo.w
    nbatch = r_ref.shape[0]
    ngroup = r_ref.shape[2] // w
    streams = [(b, gi) for b in range(nbatch) for gi in range(ngroup)]

    @pl.when(pl.program_id(1) == 0)
    def _():
        state_ref[...] = jnp.zeros_like(state_ref)

    pars = [tuple(p[:, gi * w:(gi + 1) * w] for p in (ks_ref, ki_ref, rb_ref, lnw_ref, lnb_ref))
            for _, gi in streams]
    local = [dict() for _ in range(nchunk)]
    carry = [dict() for _ in range(nchunk)]

    def tile(ref, ci, b, gi):
        return ref[b, pl.ds(ci * chunk, chunk), pl.ds(gi * w, w)].astype(F32)

    def make_local(ci):
        xs = [tuple(tile(ref, ci, b, gi) for ref in (r_ref, k_ref, v_ref, lw_ref, a_ref))
              for b, gi in streams]
        return _wkv_local(geo, xs, pars, local[ci])

    def make_carry(ci):
        states = (carry[ci - 1]["states"] if ci > 0
                  else [state_ref[i] for i in range(len(streams))])
        return _wkv_carry(geo, local[ci], states, carry[ci])

    def make_finish(ci):
        gates = [tile(g_ref, ci, b, gi) for b, gi in streams]

        def store(i, val):
            b, gi = streams[i]
            o_ref[b, pl.ds(ci * chunk, chunk), pl.ds(gi * w, w)] = val.astype(o_ref.dtype)

        return _wkv_finish(geo, local[ci], carry[ci]["o"], gates, pars, store)

    _run_interleaved(nchunk, make_local, make_carry, make_finish)
    for i, s_ in enumerate(carry[nchunk - 1]["states"]):
        state_ref[i] = s_


def _wkv(r, k, v, lw, a, gate, ks, ki, rb, lnw, lnb, *, batch, seq, heads, chunk, nchunk, ngroup):
    m, d = r.shape
    rows = nchunk * chunk
    lanes = ngroup * heads * HEAD_DIM
    tile = pl.BlockSpec((batch, rows, lanes), lambda h, c: (0, c, h))
    par = pl.BlockSpec((1, lanes), lambda h, c: (0, h))
    row = lambda x: x.reshape(1, d)
    b3 = lambda x: x.reshape(batch, seq, d)
    out = pl.pallas_call(
        functools.partial(_wkv_kernel, nchunk=nchunk, heads=heads, chunk=chunk),
        grid=(d // lanes, seq // rows),
        in_specs=[tile] * 6 + [par] * 5,
        out_specs=tile,
        out_shape=jax.ShapeDtypeStruct((batch, seq, d), BF16),
        scratch_shapes=[pltpu.VMEM((batch * ngroup, heads * HEAD_DIM, heads * HEAD_DIM), F32)],
        compiler_params=_cparams(("parallel", "arbitrary")),
        name="wkv7",
    )(b3(r), b3(k), b3(v), b3(lw), b3(a), b3(gate), row(ks), row(ki), row(rb), row(lnw), row(lnb))
    return out.reshape(m, d)


def _cast_split_kernel(x_ref, o_ref, *, split, start):
    n_in, kc = x_ref.shape
    full = split // LANES * LANES
    o_ref[:, 0:full] = x_ref[0:full, :].T.astype(o_ref.dtype)
    if split > full:
        tail = jnp.concatenate([x_ref[full:split, :], jnp.zeros((full + LANES - split, kc), F32)],
                               axis=0)
        o_ref[:, full:full + LANES] = tail.T.astype(o_ref.dtype)
        full += LANES
    if start > full:
        o_ref[:, full:start] = jnp.zeros((kc, start - full), o_ref.dtype)
    o_ref[:, start:] = x_ref[split:n_in, :].T.astype(o_ref.dtype)


def _cast_split_columns(w, split, start, kc=LANES):
    nl, kdim, n = w.shape
    n_out = start + n - split
    assert start % LANES == 0 and (n - split) % LANES == 0 and start >= _round_up(split, LANES)
    return pl.pallas_call(
        functools.partial(_cast_split_kernel, split=split, start=start),
        grid=(nl, kdim // kc),
        in_specs=[pl.BlockSpec((None, n, kc), lambda l, i: (l, 0, i))],
        out_specs=pl.BlockSpec((None, kc, n_out), lambda l, i: (l, i, 0)),
        out_shape=jax.ShapeDtypeStruct((nl, kdim, n_out), BF16),
        compiler_params=_cparams(("parallel", "parallel")),
        name="cast_w_in",
    )(jnp.swapaxes(w, 1, 2))


def _pad_to(x, size, axis):
    pad = size - x.shape[axis]
    if pad == 0:
        return x
    widths = [(0, 0)] * x.ndim
    widths[axis] = (0, pad)
    return jnp.pad(x, widths)


def _round_up(n, mult):
    return (n + mult - 1) // mult * mult


def _pick(n, prefs):
    for p in prefs:
        if n % p == 0:
            return p
    return n


def kernel(x, attn_norm, w_in, shift_mu, decay_up, decay_base, iclr_up, iclr_base, gate_up, k_scale, k_iclr, r_bonus, lnx_w, lnx_b, vres_down, vres_up, vres_base, w_out_rwkv, sconv_w, w_out_conv, w_o, ffn_norm, w_up, ffn_conv_w, w_down, final_norm):
    batch, seq, d_model = x.shape
    depth = w_in.shape[0]
    d_rwkv = decay_up.shape[2]
    d_conv = sconv_w.shape[2]
    d_ff = w_down.shape[1]
    r_dec, r_icl, r_gate, r_vres = decay_up.shape[1], iclr_up.shape[1], gate_up.shape[1], vres_down.shape[2]
    rwkv_cols = 3 * d_rwkv + r_dec + r_icl + r_gate
    conv_cols = 3 * d_conv
    m = batch * seq

    r_gate_p = _round_up(r_gate, LANES)
    r_vres_p = _round_up(r_vres, LANES)
    rwkv_cols_p = 3 * d_rwkv + r_dec + r_icl + r_gate_p

    tm = _pick(seq, (1024, 512, 256, 128))
    tn_r = _pick(rwkv_cols_p, (768, 1152, 1024, 512, 256, 128))
    tn_g = _pick(2 * d_model, (1024, 512, 256, 128))
    tn_d = _pick(d_model, (1024, 512, 256, 128))
    tn_o = _pick(d_model, (512, 256, 128))
    tc_conv = _pick(d_conv, (512, 256, 128))
    tc_ff = _pick(d_ff, (512, 256, 128))
    tn_dn = _pick(d_model, (256, 128))
    tp = _pick(seq, (128,))
    wkv_nchunk = _pick(seq // WKV_CHUNK, (4, 2, 1))
    wkv_ngroup = _pick(d_rwkv // (WKV_HEADS * HEAD_DIM), (2, 1))

    xf = x.reshape(m, d_model)
    h, ssq = _prenorm(xf, attn_norm[0])
    res_dtypes = (F32, BF16, "ssq")
    v_first = None

    conv0 = _round_up(rwkv_cols_p, tc_conv)
    while (conv0 + conv_cols) % tn_g:
        conv0 += tc_conv
    gate0 = conv0 + conv_cols
    w_in_b = _cast_split_columns(w_in, rwkv_cols, conv0)
    w_or_b, w_oc_b, w_o_b = (w.astype(BF16) for w in (w_out_rwkv, w_out_conv, w_o))
    w_up_b, w_down_b = w_up.astype(BF16), w_down.astype(BF16)

    for l in range(depth):
        p_r = _matmul(h, w_in_b, l, BF16, tm=tm, tn=tn_r, n=rwkv_cols_p, ssq=ssq, name="proj_rwkv")
        vres = None
        if l > 0:
            vres = (_pad_to(vres_down[l - 1], r_vres_p, 1).astype(BF16),
                    _pad_to(vres_up[l - 1], r_vres_p, 0).astype(BF16), vres_base[l - 1])
        r, k, v, lw, a, gate = _rwkv_prep(
            p_r, _pad_to(shift_mu[l], rwkv_cols_p, 0), decay_up[l].astype(BF16), decay_base[l],
            iclr_up[l].astype(BF16), iclr_base[l], _pad_to(gate_up[l], r_gate_p, 0).astype(BF16),
            vres, v_first, d_rwkv=d_rwkv, seq=seq, tp=tp)
        if l == 0:
            v_first = v
        o_rwkv = _wkv(r, k, v, lw, a, gate, k_scale[l], k_iclr[l], r_bonus[l].reshape(-1),
                      lnx_w[l], lnx_b[l], batch=batch, seq=seq, heads=WKV_HEADS, chunk=WKV_CHUNK,
                      nchunk=wkv_nchunk, ngroup=wkv_ngroup)

        z_conv = _matmul_conv(h, ssq, w_in_b, l, sconv_w[l], nparts=3, n=d_conv, tc=tc_conv, tm=tm,
                              seq=seq, col0=conv0, name="proj_conv")

        gates = _matmul(h, w_in_b, l, BF16, tm=tm, tn=tn_g, col0=gate0, ssq=ssq, epilogue=_sigmoid,
                        name="proj_gates")
        merged = _merge(o_rwkv, z_conv, w_or_b, w_oc_b, l, gates, tm=tm, tn=tn_d)
        xf, h, ssq = _matmul(merged, w_o_b, l, res_dtypes, tm=tm, tn=tn_o,
                             tile_extras=(xf,), row_extras=(ffn_norm[l],),
                             epilogue=_residual_epilogue, name="w_o")

        act = _matmul_conv(h, ssq, w_up_b, l, ffn_conv_w[l], nparts=2, n=d_ff, tc=tc_ff, tm=tm,
                           seq=seq, name="ffn_up")
        if l + 1 < depth:
            xf, h, ssq = _matmul(act, w_down_b, l, res_dtypes, tm=tm, tn=tn_dn,
                                 tile_extras=(xf,), row_extras=(attn_norm[l + 1],),
                                 epilogue=_residual_epilogue, name="w_down")
        else:
            xf = _matmul(act, w_down_b, l, F32, tm=tm, tn=tn_dn,
                         tile_extras=(xf,), epilogue=lambda acc, res: res + acc, name="w_down_last")
    out = _rmsnorm(xf, final_norm, F32)
    return out.reshape(batch, seq, d_model)
```

```python
import functools

import jax
import jax.numpy as jnp
from jax import lax
from jax.experimental import pallas as pl
from jax.experimental.pallas import tpu as pltpu

F32 = jnp.float32
BF16 = jnp.bfloat16

HEAD_DIM = 64
LANES = 128
WKV_HEADS = 4
WKV_CHUNK = 64
LOCAL_TASKS = 2
NORM_EPS = 1e-6
GN_EPS = 64e-5
DECAY_SCALE = 0.606531
L2_EPS = 1e-12
VMEM_LIMIT = 56 * 1024 * 1024


def _cparams(sem):
    return pltpu.CompilerParams(dimension_semantics=sem, vmem_limit_bytes=VMEM_LIMIT)


def _sigmoid(x):
    return 0.5 * jnp.tanh(0.5 * x) + 0.5


def _bdot(a, b):
    return jnp.dot(a.astype(BF16), b.astype(BF16), preferred_element_type=F32)


def _bdot_nt(a, b):
    return lax.dot_general(a.astype(BF16), b.astype(BF16), (((1,), (1,)), ((), ())),
                           preferred_element_type=F32)


def _bdot_tn(a, b):
    return lax.dot_general(a.astype(BF16), b.astype(BF16), (((0,), (0,)), ((), ())),
                           preferred_element_type=F32)


def _rmsnorm_kernel(x_ref, g_ref, o_ref):
    x = x_ref[...]
    ms = jnp.mean(x * x, axis=-1, keepdims=True)
    o_ref[...] = (x * lax.rsqrt(ms + NORM_EPS) * g_ref[...]).astype(o_ref.dtype)


def _rmsnorm(x, g, out_dtype, tm=256):
    m, d = x.shape
    return pl.pallas_call(
        _rmsnorm_kernel,
        grid=(m // tm,),
        in_specs=[pl.BlockSpec((tm, d), lambda i: (i, 0)),
                  pl.BlockSpec((1, d), lambda i: (0, 0))],
        out_specs=pl.BlockSpec((tm, d), lambda i: (i, 0)),
        out_shape=jax.ShapeDtypeStruct((m, d), out_dtype),
        compiler_params=_cparams(("parallel",)),
        name="rmsnorm",
    )(x, g.reshape(1, d))


def _row_scale(ssq, inv_d):
    return lax.rsqrt(jnp.sum(ssq, axis=-1, keepdims=True) * inv_d + NORM_EPS)


def _lane_partial_ssq(x):
    sq = x * x
    out = sq[:, 0:LANES]
    for c in range(1, x.shape[1] // LANES):
        out = out + sq[:, c * LANES:(c + 1) * LANES]
    return out


def _residual_epilogue(acc, res, gain):
    x = res + acc
    return x, x * gain, _lane_partial_ssq(x)


def _prenorm_kernel(x_ref, g_ref, xg_ref, ssq_ref):
    x = x_ref[...]
    xg_ref[...] = (x * g_ref[...]).astype(xg_ref.dtype)
    ssq_ref[...] = _lane_partial_ssq(x)


def _prenorm(x, g, tm=256):
    m, d = x.shape
    return pl.pallas_call(
        _prenorm_kernel,
        grid=(m // tm,),
        in_specs=[pl.BlockSpec((tm, d), lambda i: (i, 0)),
                  pl.BlockSpec((1, d), lambda i: (0, 0))],
        out_specs=[pl.BlockSpec((tm, d), lambda i: (i, 0)),
                   pl.BlockSpec((tm, LANES), lambda i: (i, 0))],
        out_shape=[jax.ShapeDtypeStruct((m, d), BF16), jax.ShapeDtypeStruct((m, LANES), F32)],
        compiler_params=_cparams(("parallel",)),
        name="prenorm",
    )(x, g.reshape(1, d))


def _accumulate_over_columns(o_ref, val):
    j = pl.program_id(1)

    @pl.when(j == 0)
    def _():
        o_ref[...] = val

    @pl.when(j != 0)
    def _():
        o_ref[...] += val


def _mm_kernel(a_ref, w_ref, *rest, epilogue, n_tile, n_row, acc_outs, inv_d):
    n_out = len(acc_outs)
    n_ssq = 0 if inv_d is None else 1
    ssq_ref = rest[0] if n_ssq else None
    tiles = rest[n_ssq:n_ssq + n_tile]
    rows = rest[n_ssq + n_tile:n_ssq + n_tile + n_row]
    outs = rest[n_ssq + n_tile + n_row:n_ssq + n_tile + n_row + n_out]
    acc = jnp.dot(a_ref[...], w_ref[...], preferred_element_type=F32)
    if n_ssq:
        acc = acc * _row_scale(ssq_ref[...], inv_d)
    vals = epilogue(acc, *[e[...] for e in tiles], *[e[...] for e in rows])
    vals = vals if isinstance(vals, tuple) else (vals,)
    for o_ref, val, accumulate in zip(outs, vals, acc_outs):
        if accumulate:
            _accumulate_over_columns(o_ref, val)
        else:
            o_ref[...] = val.astype(o_ref.dtype)


def _matmul(a, w, layer, out_dtypes, *, tm, tn, col0=0, n=None, ssq=None, tile_extras=(),
            row_extras=(), epilogue=None, name="mm"):
    m, kdim = a.shape
    n = w.shape[2] - col0 if n is None else n
    nj = n // tn
    jb = col0 // tn
    assert jb * tn == col0 and nj * tn == n
    if epilogue is None:
        epilogue = lambda acc: acc
    single = not isinstance(out_dtypes, (tuple, list))
    out_dtypes = (out_dtypes,) if single else tuple(out_dtypes)
    inv_d = None if ssq is None else 1.0 / kdim
    acc_outs = tuple(dt == "ssq" for dt in out_dtypes)
    kern = functools.partial(_mm_kernel, epilogue=epilogue, n_tile=len(tile_extras),
                             n_row=len(row_extras), acc_outs=acc_outs, inv_d=inv_d)
    in_specs = [pl.BlockSpec((tm, kdim), lambda i, j: (i, 0)),
                pl.BlockSpec((None, kdim, tn), lambda i, j: (layer, 0, j + jb))]
    args = [a, w]
    if ssq is not None:
        in_specs.append(pl.BlockSpec((tm, ssq.shape[1]), lambda i, j: (i, 0)))
        args.append(ssq)
    in_specs += [pl.BlockSpec((tm, tn), lambda i, j: (i, j)) for _ in tile_extras]
    in_specs += [pl.BlockSpec((1, tn), lambda i, j: (0, j)) for _ in row_extras]
    args += list(tile_extras) + [r.reshape(1, n) for r in row_extras]
    out_specs, out_shape = [], []
    for dt in out_dtypes:
        if dt == "ssq":
            out_specs.append(pl.BlockSpec((tm, LANES), lambda i, j: (i, 0)))
            out_shape.append(jax.ShapeDtypeStruct((m, LANES), F32))
        else:
            out_specs.append(pl.BlockSpec((tm, tn), lambda i, j: (i, j)))
            out_shape.append(jax.ShapeDtypeStruct((m, n), dt))
    col_sem = "arbitrary" if any(acc_outs) else "parallel"
    outs = pl.pallas_call(
        kern,
        grid=(m // tm, nj),
        in_specs=in_specs,
        out_specs=out_specs,
        out_shape=out_shape,
        compiler_params=_cparams(("parallel", col_sem)),
        name=name,
    )(*args)
    return outs[0] if single else outs


def _shift_rows(x, carry, nshift):
    rows = lax.broadcasted_iota(jnp.int32, x.shape, 0)
    out = pltpu.roll(x, nshift, 0)
    for r in range(nshift):
        src = carry[8 - nshift + r:8 - nshift + r + 1, :]
        out = jnp.where(rows == r, src, out)
    return out


def _conv3(x, carry, w_ref):
    x1 = _shift_rows(x, carry, 1)
    x2 = _shift_rows(x, carry, 2)
    return w_ref[0:1, :] * x2 + w_ref[1:2, :] * x1 + w_ref[2:3, :] * x


def _mm_conv_kernel(*refs, nparts, tiles_per_seq, inv_d):
    a_ref, ssq_ref = refs[0], refs[1]
    refs = refs[2:]
    w_refs = refs[0:nparts]
    ncw = 1 if nparts == 3 else 2
    cw_refs = refs[nparts:nparts + ncw]
    o_ref = refs[nparts + ncw]
    carry_refs = refs[1 + nparts + ncw:]
    i = pl.program_id(1)
    first = (i % tiles_per_seq) == 0
    a = a_ref[...]
    scale = _row_scale(ssq_ref[...], inv_d)
    accs = [jnp.dot(a, w[...], preferred_element_type=F32) * scale for w in w_refs]
    tm = accs[0].shape[0]
    if nparts == 3:
        pres = [accs[1] * accs[2]]
    else:
        pres = accs
    ys = []
    for pre, cw_ref, carry_ref in zip(pres, cw_refs, carry_refs):
        carry = jnp.where(first, 0.0, carry_ref[...])
        ys.append(_conv3(pre, carry, cw_ref))
        carry_ref[...] = pre[tm - 8:tm, :]
    if nparts == 3:
        out = accs[0] * ys[0]
    else:
        out = ys[0] * _sigmoid(ys[0]) * ys[1]
    o_ref[...] = out.astype(o_ref.dtype)


def _matmul_conv(a, ssq, w, layer, cw, *, nparts, n, tc, tm, seq, col0=0, name):
    m, kdim = a.shape
    nj = n // tc
    jb = col0 // tc
    assert jb * tc == col0 and nj * tc == n
    ncw = 1 if nparts == 3 else 2
    kern = functools.partial(_mm_conv_kernel, nparts=nparts, tiles_per_seq=seq // tm,
                             inv_d=1.0 / kdim)
    w_specs = [pl.BlockSpec((None, kdim, tc), lambda j, i, p=p: (layer, 0, jb + j + p * nj))
               for p in range(nparts)]
    cw_specs = [pl.BlockSpec((3, tc), lambda j, i, p=p: (0, j + p * nj)) for p in range(ncw)]
    return pl.pallas_call(
        kern,
        grid=(nj, m // tm),
        in_specs=[pl.BlockSpec((tm, kdim), lambda j, i: (i, 0)),
                  pl.BlockSpec((tm, ssq.shape[1]), lambda j, i: (i, 0))] + w_specs + cw_specs,
        out_specs=pl.BlockSpec((tm, tc), lambda j, i: (i, j)),
        out_shape=jax.ShapeDtypeStruct((m, n), BF16),
        scratch_shapes=[pltpu.VMEM((8, tc), F32)] * ncw,
        compiler_params=_cparams(("arbitrary", "arbitrary")),
        name=name,
    )(a, ssq, *([w] * nparts), *([cw] * ncw))


def _merge_kernel(o_ref, z_ref, wr_ref, wc_ref, gr_ref, gc_ref, out_ref):
    yr = jnp.dot(o_ref[...], wr_ref[...], preferred_element_type=F32)
    yc = jnp.dot(z_ref[...], wc_ref[...], preferred_element_type=F32)
    out = gr_ref[...].astype(F32) * yr + gc_ref[...].astype(F32) * yc
    out_ref[...] = out.astype(out_ref.dtype)


def _merge(o, z, wr, wc, layer, gates, *, tm, tn):
    m, kdim = o.shape
    n = wr.shape[2]
    nj = n // tn
    return pl.pallas_call(
        _merge_kernel,
        grid=(m // tm, nj),
        in_specs=[pl.BlockSpec((tm, kdim), lambda i, j: (i, 0)),
                  pl.BlockSpec((tm, kdim), lambda i, j: (i, 0)),
                  pl.BlockSpec((None, kdim, tn), lambda i, j: (layer, 0, j)),
                  pl.BlockSpec((None, kdim, tn), lambda i, j: (layer, 0, j)),
                  pl.BlockSpec((tm, tn), lambda i, j: (i, j)),
                  pl.BlockSpec((tm, tn), lambda i, j: (i, j + nj))],
        out_specs=pl.BlockSpec((tm, tn), lambda i, j: (i, j)),
        out_shape=jax.ShapeDtypeStruct((m, n), BF16),
        compiler_params=_cparams(("parallel", "parallel")),
        name="merge",
    )(o, z, wr, wc, gates, gates)


def _prep_kernel(*refs, d_rwkv, r_dec, r_icl, r_gate_p, has_vres, tiles_per_seq):
    if has_vres:
        (p_ref, mu_ref, dup_ref, dbase_ref, iup_ref, ibase_ref, gup_ref,
         vd_ref, vu_ref, vb_ref, vf_ref,
         r_out, k_out, v_out, lw_out, a_out, g_out, carry_ref) = refs
    else:
        (p_ref, mu_ref, dup_ref, dbase_ref, iup_ref, ibase_ref, gup_ref,
         r_out, k_out, v_out, lw_out, a_out, g_out, carry_ref) = refs
    i = pl.program_id(0)
    first = (i % tiles_per_seq) == 0
    tp = p_ref.shape[0]
    carry = jnp.where(first, 0.0, carry_ref[...])

    def lerp(c0, c1):
        p = p_ref[:, c0:c1].astype(F32)
        prev = _shift_rows(p, carry[:, c0:c1], 1)
        return p + mu_ref[:, c0:c1] * (prev - p)

    d = d_rwkv
    o_d = 3 * d
    o_a = o_d + r_dec
    o_g = o_a + r_icl
    r_out[...] = lerp(0, d).astype(r_out.dtype)
    k_out[...] = lerp(d, 2 * d).astype(k_out.dtype)
    v = lerp(2 * d, 3 * d)
    d_lo = lerp(o_d, o_a)
    a_lo = lerp(o_a, o_g)
    g_lo = lerp(o_g, o_g + r_gate_p)
    carry_ref[...] = p_ref[tp - 16:tp, :].astype(F32)[8:16, :]

    dec = dbase_ref[...] + _bdot(jnp.tanh(d_lo), dup_ref[...])
    lw_out[...] = -DECAY_SCALE * _sigmoid(dec)
    a_out[...] = _sigmoid(ibase_ref[...] + _bdot(a_lo, iup_ref[...])).astype(a_out.dtype)
    g_out[...] = _bdot(_sigmoid(g_lo), gup_ref[...]).astype(g_out.dtype)
    if has_vres:
        lo = _bdot(v, vd_ref[...])
        mix = _sigmoid(vb_ref[...] + _bdot(lo, vu_ref[...]))
        v = v + (vf_ref[...].astype(F32) - v) * mix
    v_out[...] = v.astype(v_out.dtype)


def _rwkv_prep(p, mu, dup, dbase, iup, ibase, gup, vres, v_first, *, d_rwkv, seq, tp=128):
    m, ncol = p.shape
    r_dec, r_icl, r_gate_p = dup.shape[0], iup.shape[0], gup.shape[0]
    has_vres = vres is not None
    row = lambda a: a.reshape(1, -1)
    full = lambda a: pl.BlockSpec(a.shape, lambda i: (0, 0))
    tile = pl.BlockSpec((tp, d_rwkv), lambda i: (i, 0))
    args = [p, row(mu), dup, row(dbase), iup, row(ibase), gup]
    in_specs = [pl.BlockSpec((tp, ncol), lambda i: (i, 0))] + [full(a) for a in args[1:]]
    if has_vres:
        vd, vu, vb = vres
        extra = [vd, vu, row(vb)]
        args += extra + [v_first]
        in_specs += [full(a) for a in extra] + [tile]
    kern = functools.partial(_prep_kernel, d_rwkv=d_rwkv, r_dec=r_dec, r_icl=r_icl,
                             r_gate_p=r_gate_p, has_vres=has_vres, tiles_per_seq=seq // tp)
    shape = lambda dt: jax.ShapeDtypeStruct((m, d_rwkv), dt)
    return pl.pallas_call(
        kern,
        grid=(m // tp,),
        in_specs=in_specs,
        out_specs=[tile] * 6,
        out_shape=[shape(BF16), shape(BF16), shape(BF16), shape(F32), shape(BF16), shape(BF16)],
        scratch_shapes=[pltpu.VMEM((8, ncol), F32)],
        compiler_params=_cparams(("arbitrary",)),
        name="rwkv_prep",
    )(*args)


def _split_bf16(x, parts):
    out = []
    for _ in range(parts):
        h = x.astype(BF16)
        out.append(h)
        x = x - h.astype(F32)
    return out


def _map(fn, *cols):
    return [fn(*args) for args in zip(*cols)]


class _WkvGeometry:
    def __init__(self, heads, chunk):
        self.g, self.c = heads, chunk
        self.w = heads * HEAD_DIM
        self.gc = heads * chunk
        c, w, gc = self.c, self.w, self.gc
        self.lane_head = lax.broadcasted_iota(jnp.int32, (c, w), 1) >> 6
        t_i = lax.broadcasted_iota(jnp.int32, (c, gc), 0)
        j_i = lax.broadcasted_iota(jnp.int32, (c, gc), 1)
        jm = j_i & (c - 1)
        self.col_head = j_i >> (c.bit_length() - 1)
        self.strict = jm < t_i
        self.incl = jm <= t_i
        self.diff = t_i ^ jm
        self.eye = (jm == t_i).astype(F32)
        tri_r = lax.broadcasted_iota(jnp.int32, (c, 4 * c), 0)
        tri_c = lax.broadcasted_iota(jnp.int32, (c, 4 * c), 1)
        self.tri = (((tri_c & (c - 1)) <= tri_r) & (tri_c < 3 * c)).astype(BF16)
        sr = lax.broadcasted_iota(jnp.int32, (w, w), 0)
        sc = lax.broadcasted_iota(jnp.int32, (w, w), 1)
        self.state_mask = (sr >> 6) == (sc >> 6)
        self.block_ones = self.state_mask.astype(BF16)

    def head_sums(self, xs):
        c = self.c
        parts = [p for x in xs for p in _split_bf16(x, 2)]
        out = jnp.dot(jnp.concatenate(parts, axis=0), self.block_ones,
                      preferred_element_type=F32)
        return [out[2 * i * c:(2 * i + 1) * c] + out[(2 * i + 1) * c:(2 * i + 2) * c]
                for i in range(len(xs))]

    def cumsum(self, x):
        parts = _split_bf16(x, 3) + [jnp.zeros(x.shape, BF16)]
        return jnp.dot(self.tri, jnp.concatenate(parts, axis=0), preferred_element_type=F32)

    def stack(self, x):
        return jnp.concatenate([jnp.where(self.lane_head == h, x, 0.0) for h in range(self.g)],
                               axis=0)

    def bd(self, n):
        return jnp.concatenate([jnp.where(self.col_head == h, n, 0.0) for h in range(self.g)],
                               axis=0)


def _wkv_local(geo, xs, pars, res):
    c, gc, w = geo.c, geo.gc, geo.w
    r, k, v, lw, a = [list(col) for col in zip(*xs)]
    ks, ki, rb = [list(col) for col in zip(*[p[:3] for p in pars])]
    kk0 = _map(lambda k_, s_: k_ * s_, k, ks)
    ss = geo.head_sums([x * x for x in kk0])
    cum = _map(geo.cumsum, lw)
    yield
    kk = _map(lambda x, s_: x / jnp.maximum(jnp.sqrt(s_), L2_EPS), kk0, ss)
    kmod = _map(lambda k_, a_, ki_: k_ * (1.0 + (a_ - 1.0) * ki_), k, a, ki)
    cmid = [x[c // 2 - 1:c // 2, :] for x in cum]
    cend = [x[c - 1:c, :] for x in cum]
    r_s = _map(lambda r_, cu, cm: r_ * jnp.exp(cu - cm), r, cum, cmid)
    a_s = _map(lambda kk_, cu, lw_, cm: -kk_ * jnp.exp(cu - lw_ - cm), kk, cum, lw, cmid)
    e_inv = _map(lambda cu, cm: jnp.exp(cm - cu), cum, cmid)
    b_s = _map(lambda kk_, a_, e: kk_ * a_ * e, kk, a, e_inv)
    k_s = _map(lambda km, e: km * e, kmod, e_inv)
    t1 = _map(lambda as_, rs_, bs_, ks_: _bdot_nt(
        jnp.concatenate([as_, rs_], axis=0),
        jnp.concatenate([geo.stack(bs_), geo.stack(ks_)], axis=0)), a_s, r_s, b_s, k_s)
    yield
    a_ab = [jnp.where(geo.strict, t[0:c, 0:gc], 0.0) for t in t1]
    a_ak = [jnp.where(geo.strict, t[0:c, gc:2 * gc], 0.0) for t in t1]
    a_rb = [jnp.where(geo.incl, t[c:2 * c, 0:gc], 0.0) for t in t1]
    a_rk = [jnp.where(geo.incl, t[c:2 * c, gc:2 * gc], 0.0) for t in t1]
    a_d = [jnp.where(geo.diff < 8, x, 0.0) for x in a_ab]
    a2 = _map(lambda x: _bdot(x, geo.bd(x)), a_d)
    yo = _map(lambda ak, rk, v_: _bdot(jnp.concatenate([ak, rk], axis=0), geo.stack(v_)),
              a_ak, a_rk, v)
    yield
    x1 = [geo.eye + x for x in a_d]
    sq = _map(lambda x, p: _bdot(jnp.concatenate([x, p], axis=0), geo.bd(p)), x1, a2)
    yield
    x2 = _map(lambda x, s_: x + s_[0:c], x1, sq)
    inv = _map(lambda x, s_: x + _bdot(x, geo.bd(s_[c:2 * c])), x2, sq)
    yield
    s = 8
    while s < c:
        shift = s.bit_length() - 1
        tmp = _map(lambda x, ab: _bdot(x, geo.bd(jnp.where((geo.diff >> shift) == 1, ab, 0.0))),
                   inv, a_ab)
        yield
        inv = _map(lambda x, t: x + _bdot(t, geo.bd(x)), inv, tmp)
        yield
        s *= 2
    w12 = _map(lambda x, as_, yo_: _bdot(
        x, jnp.concatenate([geo.stack(as_), geo.stack(yo_[0:c])], axis=1)), inv, a_s, yo)
    yield
    res.update(r=r, v=v, kmod=kmod, rb=rb, r_s=r_s, b_s=b_s, k_s=k_s, a_rb=a_rb,
               o_loc=[x[c:2 * c] for x in yo], w1=[x[:, 0:w] for x in w12],
               w2=[x[:, w:2 * w] for x in w12], cmid=cmid, cend=cend)


def _wkv_carry(geo, loc, states, res):
    c = geo.c
    s_mid = _map(lambda s_, cm: s_ * jnp.exp(cm), states, loc["cmid"])
    d1 = _map(lambda w1, rs_, sm: _bdot_nt(jnp.concatenate([w1, rs_], axis=0), sm),
              loc["w1"], loc["r_s"], s_mid)
    yield
    u = _map(lambda d, w2: d[0:c] + w2, d1, loc["w2"])
    o = _map(lambda d, arb, u_, ol: d[c:2 * c] + _bdot(arb, geo.stack(u_)) + ol,
             d1, loc["a_rb"], u, loc["o_loc"])
    gm = _map(lambda u_, v_, bs_, ks_: _bdot_tn(jnp.concatenate([u_, v_], axis=0),
                                                jnp.concatenate([bs_, ks_], axis=0)),
              u, loc["v"], loc["b_s"], loc["k_s"])
    yield
    new = _map(lambda sm, g_, ce, cm: (sm + jnp.where(geo.state_mask, g_, 0.0)) * jnp.exp(ce - cm),
               s_mid, gm, loc["cend"], loc["cmid"])
    res.update(o=o, states=new)


def _wkv_finish(geo, loc, o, gates, pars, store):
    inv_n = 1.0 / HEAD_DIM
    n = len(o)
    sums = geo.head_sums(o + _map(lambda r_, km, rb_: r_ * km * rb_, loc["r"], loc["kmod"], loc["rb"]))
    mean = [x * inv_n for x in sums[:n]]
    bonus = sums[n:]
    yield
    dlt = _map(lambda x, m_: x - m_, o, mean)
    var = [x * inv_n for x in geo.head_sums([x * x for x in dlt])]
    yield
    for i, (d, vr, bn, v_, g_, p) in enumerate(zip(dlt, var, bonus, loc["v"], gates, pars)):
        store(i, (d * lax.rsqrt(vr + GN_EPS) * p[3] + p[4] + bn * v_) * g_)


def _run_interleaved(nchunk, make_local, make_carry, make_finish):
    nlocal = min(LOCAL_TASKS, nchunk)
    active = [("local", ci, make_local(ci)) for ci in range(nlocal)]
    next_local = nlocal
    local_done, carry_done, carry_started = set(), set(), set()
    while active:
        still = []
        for kind, ci, gen in active:
            try:
                next(gen)
                still.append((kind, ci, gen))
            except StopIteration:
                if kind == "local":
                    local_done.add(ci)
                    if next_local < nchunk:
                        still.append(("local", next_local, make_local(next_local)))
                        next_local += 1
                elif kind == "carry":
                    carry_done.add(ci)
                    still.append(("finish", ci, make_finish(ci)))
        for ci in range(nchunk):
            if (ci in local_done and ci not in carry_started
                    and (ci == 0 or ci - 1 in carry_done)):
                carry_started.add(ci)
                still.append(("carry", ci, make_carry(ci)))
        active = still


def _wkv_kernel(r_ref, k_ref, v_ref, lw_ref, a_ref, g_ref, ks_ref, ki_ref, rb_ref, lnw_ref,
                lnb_ref, o_ref, state_ref, *, nchunk, heads, chunk):
    geo = _WkvGeometry(heads, chunk)
    w = geo.w
    nbatch = r_ref.shape[0]
    ngroup = r_ref.shape[2] // w
    streams = [(b, gi) for b in range(nbatch) for gi in range(ngroup)]

    @pl.when(pl.program_id(1) == 0)
    def _():
        state_ref[...] = jnp.zeros_like(state_ref)

    pars = [tuple(p[:, gi * w:(gi + 1) * w] for p in (ks_ref, ki_ref, rb_ref, lnw_ref, lnb_ref))
            for _, gi in streams]
    local = [dict() for _ in range(nchunk)]
    carry = [dict() for _ in range(nchunk)]

    def tile(ref, ci, b, gi):
        return ref[b, pl.ds(ci * chunk, chunk), pl.ds(gi * w, w)].astype(F32)

    def make_local(ci):
        xs = [tuple(tile(ref, ci, b, gi) for ref in (r_ref, k_ref, v_ref, lw_ref, a_ref))
              for b, gi in streams]
        return _wkv_local(geo, xs, pars, local[ci])

    def make_carry(ci):
        states = (carry[ci - 1]["states"] if ci > 0
                  else [state_ref[i] for i in range(len(streams))])
        return _wkv_carry(geo, local[ci], states, carry[ci])

    def make_finish(ci):
        gates = [tile(g_ref, ci, b, gi) for b, gi in streams]

        def store(i, val):
            b, gi = streams[i]
            o_ref[b, pl.ds(ci * chunk, chunk), pl.ds(gi * w, w)] = val.astype(o_ref.dtype)

        return _wkv_finish(geo, local[ci], carry[ci]["o"], gates, pars, store)

    _run_interleaved(nchunk, make_local, make_carry, make_finish)
    for i, s_ in enumerate(carry[nchunk - 1]["states"]):
        state_ref[i] = s_


def _wkv(r, k, v, lw, a, gate, ks, ki, rb, lnw, lnb, *, batch, seq, heads, chunk, nchunk, ngroup):
    m, d = r.shape
    rows = nchunk * chunk
    lanes = ngroup * heads * HEAD_DIM
    tile = pl.BlockSpec((batch, rows, lanes), lambda h, c: (0, c, h))
    par = pl.BlockSpec((1, lanes), lambda h, c: (0, h))
    row = lambda x: x.reshape(1, d)
    b3 = lambda x: x.reshape(batch, seq, d)
    out = pl.pallas_call(
        functools.partial(_wkv_kernel, nchunk=nchunk, heads=heads, chunk=chunk),
        grid=(d // lanes, seq // rows),
        in_specs=[tile] * 6 + [par] * 5,
        out_specs=tile,
        out_shape=jax.ShapeDtypeStruct((batch, seq, d), BF16),
        scratch_shapes=[pltpu.VMEM((batch * ngroup, heads * HEAD_DIM, heads * HEAD_DIM), F32)],
        compiler_params=_cparams(("parallel", "arbitrary")),
        name="wkv7",
    )(b3(r), b3(k), b3(v), b3(lw), b3(a), b3(gate), row(ks), row(ki), row(rb), row(lnw), row(lnb))
    return out.reshape(m, d)


def _cast_split_kernel(x_ref, o_ref, *, split, start):
    n_in, kc = x_ref.shape
    full = split // LANES * LANES
    o_ref[:, 0:full] = x_ref[0:full, :].T.astype(o_ref.dtype)
    if split > full:
        tail = jnp.concatenate([x_ref[full:split, :], jnp.zeros((full + LANES - split, kc), F32)],
                               axis=0)
        o_ref[:, full:full + LANES] = tail.T.astype(o_ref.dtype)
        full += LANES
    if start > full:
        o_ref[:, full:start] = jnp.zeros((kc, start - full), o_ref.dtype)
    o_ref[:, start:] = x_ref[split:n_in, :].T.astype(o_ref.dtype)


def _cast_split_columns(w, split, start, kc=LANES):
    nl, kdim, n = w.shape
    n_out = start + n - split
    assert start % LANES == 0 and (n - split) % LANES == 0 and start >= _round_up(split, LANES)
    return pl.pallas_call(
        functools.partial(_cast_split_kernel, split=split, start=start),
        grid=(nl, kdim // kc),
        in_specs=[pl.BlockSpec((None, n, kc), lambda l, i: (l, 0, i))],
        out_specs=pl.BlockSpec((None, kc, n_out), lambda l, i: (l, i, 0)),
        out_shape=jax.ShapeDtypeStruct((nl, kdim, n_out), BF16),
        compiler_params=_cparams(("parallel", "parallel")),
        name="cast_w_in",
    )(jnp.swapaxes(w, 1, 2))


def _pad_to(x, size, axis):
    pad = size - x.shape[axis]
    if pad == 0:
        return x
    widths = [(0, 0)] * x.ndim
    widths[axis] = (0, pad)
    return jnp.pad(x, widths)


def _round_up(n, mult):
    return (n + mult - 1) // mult * mult


def _pick(n, prefs):
    for p in prefs:
        if n % p == 0:
            return p
    return n


def kernel(x, attn_norm, w_in, shift_mu, decay_up, decay_base, iclr_up, iclr_base, gate_up, k_scale, k_iclr, r_bonus, lnx_w, lnx_b, vres_down, vres_up, vres_base, w_out_rwkv, sconv_w, w_out_conv, w_o, ffn_norm, w_up, ffn_conv_w, w_down, final_norm):
    batch, seq, d_model = x.shape
    depth = w_in.shape[0]
    d_rwkv = decay_up.shape[2]
    d_conv = sconv_w.shape[2]
    d_ff = w_down.shape[1]
    r_dec, r_icl, r_gate, r_vres = decay_up.shape[1], iclr_up.shape[1], gate_up.shape[1], vres_down.shape[2]
    rwkv_cols = 3 * d_rwkv + r_dec + r_icl + r_gate
    conv_cols = 3 * d_conv
    m = batch * seq

    r_gate_p = _round_up(r_gate, LANES)
    r_vres_p = _round_up(r_vres, LANES)
    rwkv_cols_p = 3 * d_rwkv + r_dec + r_icl + r_gate_p

    tm = _pick(seq, (1024, 512, 256, 128))
    tn_r = _pick(rwkv_cols_p, (768, 1152, 1024, 512, 256, 128))
    tn_g = _pick(2 * d_model, (1024, 512, 256, 128))
    tn_d = _pick(d_model, (1024, 512, 256, 128))
    tn_o = _pick(d_model, (512, 256, 128))
    tc_conv = _pick(d_conv, (512, 256, 128))
    tc_ff = _pick(d_ff, (512, 256, 128))
    tn_dn = _pick(d_model, (256, 128))
    tp = _pick(seq, (128,))
    wkv_nchunk = _pick(seq // WKV_CHUNK, (4, 2, 1))
    wkv_ngroup = _pick(d_rwkv // (WKV_HEADS * HEAD_DIM), (2, 1))

    xf = x.reshape(m, d_model)
    h, ssq = _prenorm(xf, attn_norm[0])
    res_dtypes = (F32, BF16, "ssq")
    v_first = None

    conv0 = _round_up(rwkv_cols_p, tc_conv)
    while (conv0 + conv_cols) % tn_g:
        conv0 += tc_conv
    gate0 = conv0 + conv_cols
    w_in_b = _cast_split_columns(w_in, rwkv_cols, conv0)
    w_or_b, w_oc_b, w_o_b = (w.astype(BF16) for w in (w_out_rwkv, w_out_conv, w_o))
    w_up_b, w_down_b = w_up.astype(BF16), w_down.astype(BF16)

    for l in range(depth):
        p_r = _matmul(h, w_in_b, l, BF16, tm=tm, tn=tn_r, n=rwkv_cols_p, ssq=ssq, name="proj_rwkv")
        vres = None
        if l > 0:
            vres = (_pad_to(vres_down[l - 1], r_vres_p, 1).astype(BF16),
                    _pad_to(vres_up[l - 1], r_vres_p, 0).astype(BF16), vres_base[l - 1])
        r, k, v, lw, a, gate = _rwkv_prep(
            p_r, _pad_to(shift_mu[l], rwkv_cols_p, 0), decay_up[l].astype(BF16), decay_base[l],
            iclr_up[l].astype(BF16), iclr_base[l], _pad_to(gate_up[l], r_gate_p, 0).astype(BF16),
            vres, v_first, d_rwkv=d_rwkv, seq=seq, tp=tp)
        if l == 0:
            v_first = v
        o_rwkv = _wkv(r, k, v, lw, a, gate, k_scale[l], k_iclr[l], r_bonus[l].reshape(-1),
                      lnx_w[l], lnx_b[l], batch=batch, seq=seq, heads=WKV_HEADS, chunk=WKV_CHUNK,
                      nchunk=wkv_nchunk, ngroup=wkv_ngroup)

        z_conv = _matmul_conv(h, ssq, w_in_b, l, sconv_w[l], nparts=3, n=d_conv, tc=tc_conv, tm=tm,
                              seq=seq, col0=conv0, name="proj_conv")

        gates = _matmul(h, w_in_b, l, BF16, tm=tm, tn=tn_g, col0=gate0, ssq=ssq, epilogue=_sigmoid,
                        name="proj_gates")
        merged = _merge(o_rwkv, z_conv, w_or_b, w_oc_b, l, gates, tm=tm, tn=tn_d)
        xf, h, ssq = _matmul(merged, w_o_b, l, res_dtypes, tm=tm, tn=tn_o,
                             tile_extras=(xf,), row_extras=(ffn_norm[l],),
                             epilogue=_residual_epilogue, name="w_o")

        act = _matmul_conv(h, ssq, w_up_b, l, ffn_conv_w[l], nparts=2, n=d_ff, tc=tc_ff, tm=tm,
                           seq=seq, name="ffn_up")
        if l + 1 < depth:
            xf, h, ssq = _matmul(act, w_down_b, l, res_dtypes, tm=tm, tn=tn_dn,
                                 tile_extras=(xf,), row_extras=(attn_norm[l + 1],),
                                 epilogue=_residual_epilogue, name="w_down")
        else:
            xf = _matmul(act, w_down_b, l, F32, tm=tm, tn=tn_dn,
                         tile_extras=(xf,), epilogue=lambda acc, res: res + acc, name="w_down_last")
    out = _rmsnorm(xf, final_norm, F32)
    return out.reshape(batch, seq, d_model)
```
